```python
import math
import jax, jax.numpy as jnp
from jax import lax
import numpy as np

D_MODEL = 1024
BATCH = 16
SEQ = 2048
DEPTH = 4

N_MIXERS = 2
QK_DIM = 64
V_DIM = 2 * QK_DIM
N_HEADS = D_MODEL // V_DIM
ROPE_DIM = QK_DIM // 4
ROPE_THETA = 500000.0
Q_BLOCK = 128
POOL_WINDOWS = (2, 4, 8, 16)
N_POOL_GROUPS = len(POOL_WINDOWS)
POOL_GROUP = D_MODEL // N_POOL_GROUPS
D_FF = 4 * D_MODEL
EPS = 1e-6
LAMBDA_STD = 0.1

kernel_name = "hybrid_diffattn_multiscale_pool_encoder"


def rmsnorm(x, g):
    xf = x.astype(jnp.float32)
    y = xf * lax.rsqrt(jnp.mean(xf * xf, axis=-1, keepdims=True) + EPS) * g.astype(jnp.float32)
    return y.astype(x.dtype)


def lambda_init_fn(layer_idx):
    return 0.8 - 0.6 * math.exp(-0.3 * layer_idx)


def rope_tables(positions):
    inv_freq = ROPE_THETA ** (-jnp.arange(0, ROPE_DIM, 2, dtype=jnp.float32) / ROPE_DIM)
    ang = positions.astype(jnp.float32)[..., None] * inv_freq
    return jnp.cos(ang)[:, None], jnp.sin(ang)[:, None]


def partial_rope(x, cos, sin):
    half = ROPE_DIM // 2
    xf = x.astype(jnp.float32)
    x1, x2, xp = xf[..., :half], xf[..., half:ROPE_DIM], xf[..., ROPE_DIM:]
    rot = jnp.concatenate([x1 * cos - x2 * sin, x2 * cos + x1 * sin, xp], axis=-1)
    return rot.astype(x.dtype)


def diff_attention(h, cos, sin, w_qkv, w_o, q_gain, k_gain, lam_q1, lam_k1, lam_q2, lam_k2,
                   sub_gain, lambda_init):
    B, S, D = h.shape
    qkv = h @ w_qkv
    q, k, v = qkv[..., :D], qkv[..., D:2 * D], qkv[..., 2 * D:]
    q = q.reshape(B, S, N_HEADS, 2, QK_DIM).transpose(3, 0, 2, 1, 4)
    k = k.reshape(B, S, N_HEADS, 2, QK_DIM).transpose(3, 0, 2, 1, 4)
    v = v.reshape(B, S, N_HEADS, V_DIM).transpose(0, 2, 1, 3)
    q = partial_rope(rmsnorm(q, q_gain), cos, sin)
    k = partial_rope(rmsnorm(k, k_gain), cos, sin)
    scale = 1.0 / math.sqrt(QK_DIM)
    lam = (jnp.exp(jnp.sum(lam_q1.astype(jnp.float32) * lam_k1.astype(jnp.float32)))
           - jnp.exp(jnp.sum(lam_q2.astype(jnp.float32) * lam_k2.astype(jnp.float32)))
           + lambda_init)
    n_blk = S // Q_BLOCK
    qb = q.reshape(2, B, N_HEADS, n_blk, Q_BLOCK, QK_DIM).transpose(3, 0, 1, 2, 4, 5)

    def attend_block(q_blk):
        s = jnp.einsum('mbhqd,mbhkd->mbhqk', q_blk, k).astype(jnp.float32) * scale
        p = jax.nn.softmax(s, axis=-1)
        a = p[0] - lam * p[1]
        return jnp.einsum('bhqk,bhkv->bhqv', a.astype(v.dtype), v)

    o = lax.map(attend_block, qb)
    o = o.transpose(1, 2, 0, 3, 4).reshape(B, N_HEADS, S, V_DIM)
    o = rmsnorm(o, sub_gain) * (1.0 - lambda_init)
    o = o.transpose(0, 2, 1, 3).reshape(B, S, N_HEADS * V_DIM)
    return o @ w_o


def multiscale_pool(h, w_group, ch_scale):
    B, S, D = h.shape
    hf = h.astype(jnp.float32)
    cs = jnp.concatenate([jnp.zeros((B, 1, D), jnp.float32), jnp.cumsum(hf, axis=1)], axis=1)
    t = jnp.arange(S)
    pooled = []
    for g, w in enumerate(POOL_WINDOWS):
        lo = jnp.clip(t - w // 2, 0, S)
        hi = jnp.clip(t + w // 2, 0, S)
        csg = cs[..., g * POOL_GROUP:(g + 1) * POOL_GROUP]
        tot = jnp.take(csg, hi, axis=1) - jnp.take(csg, lo, axis=1)
        cnt = (hi - lo).astype(jnp.float32)[None, :, None]
        pooled.append(tot / cnt)
    mean = jnp.concatenate(pooled, axis=-1)
    diff = (mean - hf).astype(h.dtype).reshape(B, S, N_POOL_GROUPS, POOL_GROUP)
    y = jnp.einsum('bsgc,gcd->bsgd', diff, w_group).reshape(B, S, D)
    return y * ch_scale


def sqrelu_mlp(h, w1, w2):
    return jnp.square(jax.nn.relu(h @ w1)) @ w2


def setup_inputs(seed: int = 0) -> dict:
    key = jax.random.key(seed)
    ks = jax.random.split(key, 20)
    n_attn = (DEPTH + N_MIXERS - 1) // N_MIXERS
    n_pool = DEPTH // N_MIXERS
    f32 = jnp.float32
    nrm = lambda k, shape, s: jax.random.normal(k, shape, f32) * s
    x = jax.random.normal(ks[0], (BATCH, SEQ, D_MODEL), f32)
    offsets = jax.random.randint(ks[1], (BATCH, 1), 0, 4096, dtype=jnp.int32)
    positions = (offsets + jnp.arange(SEQ, dtype=jnp.int32)[None, :]).astype(jnp.int32)
    return {
        "x": x,
        "positions": positions,
        "norm_mix": 1.0 + nrm(ks[2], (DEPTH, D_MODEL), 0.05),
        "norm_mlp": 1.0 + nrm(ks[3], (DEPTH, D_MODEL), 0.05),
        "attn_w_qkv": nrm(ks[4], (n_attn, D_MODEL, 3 * D_MODEL), D_MODEL ** -0.5),
        "attn_w_o": nrm(ks[5], (n_attn, N_HEADS * V_DIM, D_MODEL), (N_HEADS * V_DIM) ** -0.5),
        "attn_q_gain": 1.0 + nrm(ks[6], (n_attn, QK_DIM), 0.05),
        "attn_k_gain": 1.0 + nrm(ks[7], (n_attn, QK_DIM), 0.05),
        "attn_lam_q1": nrm(ks[8], (n_attn, QK_DIM), LAMBDA_STD),
        "attn_lam_k1": nrm(ks[9], (n_attn, QK_DIM), LAMBDA_STD),
        "attn_lam_q2": nrm(ks[10], (n_attn, QK_DIM), LAMBDA_STD),
        "attn_lam_k2": nrm(ks[11], (n_attn, QK_DIM), LAMBDA_STD),
        "attn_sub_gain": 1.0 + nrm(ks[12], (n_attn, V_DIM), 0.05),
        "pool_w": nrm(ks[13], (n_pool, N_POOL_GROUPS, POOL_GROUP, POOL_GROUP), POOL_GROUP ** -0.5),
        "pool_scale": 0.5 + nrm(ks[14], (n_pool, D_MODEL), 0.05),
        "mlp_w1": nrm(ks[15], (DEPTH, D_MODEL, D_FF), D_MODEL ** -0.5),
        "mlp_w2": nrm(ks[16], (DEPTH, D_FF, D_MODEL), D_FF ** -0.5),
    }


def reference(x, positions, norm_mix, norm_mlp, attn_w_qkv, attn_w_o, attn_q_gain, attn_k_gain,
              attn_lam_q1, attn_lam_k1, attn_lam_q2, attn_lam_k2, attn_sub_gain,
              pool_w, pool_scale, mlp_w1, mlp_w2):
    cos, sin = rope_tables(positions)
    for i in range(DEPTH):
        j = i // N_MIXERS
        h = rmsnorm(x, norm_mix[i])
        if i % N_MIXERS == 0:
            y = diff_attention(h, cos, sin, attn_w_qkv[j], attn_w_o[j], attn_q_gain[j], attn_k_gain[j],
                               attn_lam_q1[j], attn_lam_k1[j], attn_lam_q2[j], attn_lam_k2[j],
                               attn_sub_gain[j], lambda_init_fn(i))
        else:
            y = multiscale_pool(h, pool_w[j], pool_scale[j])
        x = x + y.astype(x.dtype)
        h = rmsnorm(x, norm_mlp[i])
        x = x + sqrelu_mlp(h, mlp_w1[i], mlp_w2[i]).astype(x.dtype)
    return x
```

```python
import functools
import math

import numpy as np
import jax
import jax.numpy as jnp
from jax import lax
from jax.experimental import pallas as pl
from jax.experimental.pallas import tpu as pltpu

D_MODEL = 1024
DEPTH = 4
N_MIXERS = 2
QK_DIM = 64
V_DIM = 2 * QK_DIM
N_HEADS = D_MODEL // V_DIM
ROPE_DIM = QK_DIM // 4
ROPE_THETA = 500000.0
POOL_WINDOWS = (2, 4, 8, 16)
POOL_GROUP = D_MODEL // len(POOL_WINDOWS)
D_FF = 4 * D_MODEL
EPS = 1e-6

LANES = 128
POOL_HALO = 8
VMEM_LIMIT = 48 * 1024 * 1024

TM_PROJ = 512
TQ_ATTN = 256
TM_ROPE = 2048

F32 = jnp.float32
BF16 = jnp.bfloat16


def _lambda_init(layer_idx):
    return 0.8 - 0.6 * math.exp(-0.3 * layer_idx)


def _rms(x, g):
    ms = jnp.mean(x * x, axis=-1, keepdims=True)
    return x * lax.rsqrt(ms + EPS) * g


def _params(*sem):
    return pltpu.CompilerParams(dimension_semantics=sem, vmem_limit_bytes=VMEM_LIMIT)


def _resident(block_shape, index_map):
    return pl.BlockSpec(block_shape, index_map, pipeline_mode=pl.Buffered(1))


def _rope_kernel(pos_ref, invf_ref, cos_ref, sin_ref):
    ang = pos_ref[...].astype(F32) * invf_ref[...]
    cos_ref[...] = jnp.cos(ang)
    sin_ref[...] = jnp.sin(ang)


def _rope_tables(positions):
    n = positions.size
    inv_freq = ROPE_THETA ** (-jnp.arange(0, ROPE_DIM, 2, dtype=F32) / ROPE_DIM)
    half = ROPE_DIM // 2
    per_map = jnp.concatenate([inv_freq, inv_freq, jnp.zeros((QK_DIM - 2 * half,), F32)])
    invf = jnp.tile(per_map, LANES // QK_DIM).reshape(1, LANES)
    pos = positions.reshape(n, 1)
    out = jax.ShapeDtypeStruct((n, LANES), F32)
    return pl.pallas_call(
        _rope_kernel,
        grid=(n // TM_ROPE,),
        in_specs=[pl.BlockSpec((TM_ROPE, 1), lambda i: (i, 0)),
                  pl.BlockSpec((1, LANES), lambda i: (0, 0))],
        out_specs=[pl.BlockSpec((TM_ROPE, LANES), lambda i: (i, 0))] * 2,
        out_shape=[out, out],
        compiler_params=_params("parallel"),
        name="rope_tables",
    )(pos, invf)


def _head_helper_matrices():
    lane = np.arange(LANES)
    gsum = (lane[:, None] // QK_DIM == lane[None, :] // QK_DIM).astype(np.float32)
    half = ROPE_DIM // 2
    swap = np.zeros((LANES, LANES), np.float32)
    for j in range(LANES):
        d = j % QK_DIM
        if d < half:
            swap[j + half, j] = -1.0
        elif d < ROPE_DIM:
            swap[j - half, j] = 1.0
    return jnp.asarray(gsum, BF16), jnp.asarray(swap, BF16)


def _qkv_kernel(x_ref, g_ref, w_ref, qg_ref, kg_ref, cos_ref, sin_ref, gsum_ref, swap_ref,
                q_ref, k_ref, v_ref):
    h = _rms(x_ref[...], g_ref[...]).astype(BF16)
    cos = cos_ref[...]
    sin = sin_ref[...]
    gsum = gsum_ref[...]
    swap = swap_ref[...]
    for sec, (out_ref, gain_ref) in enumerate(((q_ref, qg_ref), (k_ref, kg_ref))):
        y = jnp.dot(h, w_ref[:, sec * D_MODEL:(sec + 1) * D_MODEL], preferred_element_type=F32)
        gain = gain_ref[...]
        for hd in range(N_HEADS):
            cols = slice(hd * LANES, (hd + 1) * LANES)
            yh = y[:, cols]
            ss = jnp.dot((yh * yh).astype(BF16), gsum, preferred_element_type=F32)
            inv = lax.rsqrt(ss * (1.0 / QK_DIM) + EPS)
            yg = yh * gain
            sw = jnp.dot(yg.astype(BF16), swap, preferred_element_type=F32)
            out_ref[:, cols] = ((yg * inv) * cos + (sw * inv) * sin).astype(BF16)
    y = jnp.dot(h, w_ref[:, 2 * D_MODEL:3 * D_MODEL], preferred_element_type=F32)
    v_ref[...] = y.astype(BF16)


def _qkv_proj(x, g, w, q_gain, k_gain, cos, sin, gsum, swap):
    n = x.shape[0]
    tm = TM_PROJ
    tok = lambda i: (i, 0)
    const = lambda i: (0, 0)
    out = jax.ShapeDtypeStruct((n, D_MODEL), BF16)
    return pl.pallas_call(
        _qkv_kernel,
        grid=(n // tm,),
        in_specs=[pl.BlockSpec((tm, D_MODEL), tok),
                  pl.BlockSpec((1, D_MODEL), const),
                  _resident((D_MODEL, 3 * D_MODEL), const),
                  pl.BlockSpec((1, LANES), const),
                  pl.BlockSpec((1, LANES), const),
                  pl.BlockSpec((tm, LANES), tok),
                  pl.BlockSpec((tm, LANES), tok),
                  pl.BlockSpec((LANES, LANES), const),
                  pl.BlockSpec((LANES, LANES), const)],
        out_specs=[pl.BlockSpec((tm, D_MODEL), tok)] * 3,
        out_shape=[out, out, out],
        compiler_params=_params("parallel"),
        name="qkv_proj",
    )(x, g, w, q_gain, k_gain, cos, sin, gsum, swap)


def _attn_kernel(lam_ref, q_ref, k_ref, v_ref, sg_ref, o_ref, *, lambda_init):
    q = q_ref[...]
    k = k_ref[...]
    v = v_ref[...]
    lane = lax.broadcasted_iota(jnp.int32, q.shape, 1)
    zero = jnp.zeros_like(q)
    nt = (((1,), (1,)), ((), ()))
    s1 = lax.dot_general(jnp.where(lane < QK_DIM, q, zero), k, nt, preferred_element_type=F32)
    s2 = lax.dot_general(jnp.where(lane >= QK_DIM, q, zero), k, nt, preferred_element_type=F32)

    lv = lam_ref[...]
    lam = (jnp.exp(jnp.sum(lv[0:1] * lv[1:2], axis=-1, keepdims=True))
           - jnp.exp(jnp.sum(lv[2:3] * lv[3:4], axis=-1, keepdims=True)) + lambda_init)

    def softmax_parts(s):
        p = jnp.exp(s - jnp.max(s, axis=-1, keepdims=True))
        return p, jnp.sum(p, axis=-1, keepdims=True)

    p1, l1 = softmax_parts(s1)
    p2, l2 = softmax_parts(s2)
    a = p1 * (1.0 / l1) - p2 * (lam / l2)
    o = jnp.dot(a.astype(BF16), v, preferred_element_type=F32)
    o = _rms(o, sg_ref[...]) * (1.0 - lambda_init)
    o_ref[...] = o.astype(BF16)


def _attention(lam_vecs, q, k, v, sub_gain, lambda_init, batch, seq):
    tq = TQ_ATTN
    n_q = seq // tq
    q_map = lambda b, h, i: (b * n_q + i, h)
    kv_map = lambda b, h, i: (b, h)
    return pl.pallas_call(
        functools.partial(_attn_kernel, lambda_init=lambda_init),
        grid=(batch, N_HEADS, n_q),
        in_specs=[pl.BlockSpec((4, QK_DIM), lambda b, h, i: (0, 0)),
                  pl.BlockSpec((tq, LANES), q_map),
                  pl.BlockSpec((seq, LANES), kv_map),
                  pl.BlockSpec((seq, LANES), kv_map),
                  pl.BlockSpec((1, V_DIM), lambda b, h, i: (0, 0))],
        out_specs=pl.BlockSpec((tq, LANES), q_map),
        out_shape=jax.ShapeDtypeStruct((batch * seq, D_MODEL), BF16),
        compiler_params=_params("parallel", "parallel", "parallel"),
        name="diff_attention",
    )(lam_vecs, q, k, v, sub_gain)


def _oproj_kernel(x_ref, o_ref, w_ref, out_ref):
    out_ref[...] = x_ref[...] + jnp.dot(o_ref[...], w_ref[...], preferred_element_type=F32)


def _out_proj(x, o, w):
    n = x.shape[0]
    tm = TM_PROJ
    tok = lambda i: (i, 0)
    return pl.pallas_call(
        _oproj_kernel,
        grid=(n // tm,),
        in_specs=[pl.BlockSpec((tm, D_MODEL), tok),
                  pl.BlockSpec((tm, D_MODEL), tok),
                  _resident((D_MODEL, D_MODEL), lambda i: (0, 0))],
        out_specs=pl.BlockSpec((tm, D_MODEL), tok),
        out_shape=jax.ShapeDtypeStruct((n, D_MODEL), F32),
        compiler_params=_params("parallel"),
        name="attn_out_proj",
    )(x, o, w)


def _pool_kernel(xp_ref, x_ref, xn_ref, g_ref, w_ref, sc_ref, out_ref, hh_ref, *, seq, tm):
    tiles_per_seq = seq // tm
    j = pl.program_id(0) % tiles_per_seq
    g = g_ref[...]
    x = x_ref[...]
    h = _rms(x, g)
    hp = jnp.where(j > 0, _rms(xp_ref[...], g), 0.0)
    hn = jnp.where(j < tiles_per_seq - 1, _rms(xn_ref[...], g), 0.0)
    hh_ref[0:POOL_HALO, :] = hp
    hh_ref[POOL_HALO:POOL_HALO + tm, :] = h
    hh_ref[POOL_HALO + tm:, :] = hn
    t = j * tm + lax.broadcasted_iota(jnp.int32, (tm, 1), 0)
    sc = sc_ref[...]
    for gi, win in enumerate(POOL_WINDOWS):
        r = win // 2
        cols = slice(gi * POOL_GROUP, (gi + 1) * POOL_GROUP)
        tot = hh_ref[pl.ds(POOL_HALO - r, tm), cols]
        for off in range(-r + 1, r):
            tot = tot + hh_ref[pl.ds(POOL_HALO + off, tm), cols]
        cnt = (jnp.minimum(t + r, seq) - jnp.maximum(t - r, 0)).astype(F32)
        diff = (tot / cnt - h[:, cols]).astype(BF16)
        y = jnp.dot(diff, w_ref[gi], preferred_element_type=F32)
        out_ref[:, cols] = x[:, cols] + y * sc[:, cols]


def _pool_mixer(x, g, w, scale, seq):
    n = x.shape[0]
    tm = TM_PROJ
    per_tile = tm // POOL_HALO
    last_halo_block = n // POOL_HALO - 1
    tok = lambda i: (i, 0)
    const = lambda i: (0, 0)
    return pl.pallas_call(
        functools.partial(_pool_kernel, seq=seq, tm=tm),
        grid=(n // tm,),
        in_specs=[pl.BlockSpec((POOL_HALO, D_MODEL), lambda i: (jnp.maximum(i * per_tile - 1, 0), 0)),
                  pl.BlockSpec((tm, D_MODEL), tok),
                  pl.BlockSpec((POOL_HALO, D_MODEL),
                               lambda i: (jnp.minimum((i + 1) * per_tile, last_halo_block), 0)),
                  pl.BlockSpec((1, D_MODEL), const),
                  _resident((len(POOL_WINDOWS), POOL_GROUP, POOL_GROUP), lambda i: (0, 0, 0)),
                  pl.BlockSpec((1, D_MODEL), const)],
        out_specs=pl.BlockSpec((tm, D_MODEL), tok),
        out_shape=jax.ShapeDtypeStruct((n, D_MODEL), F32),
        scratch_shapes=[pltpu.VMEM((tm + 2 * POOL_HALO, D_MODEL), F32)],
        compiler_params=_params("parallel"),
        name="pool_mixer",
    )(x, x, x, g, w, scale)


def _mlp_kernel(x_ref, g_ref, w1_ref, w2_ref, out_ref):
    x = x_ref[...]
    h = _rms(x, g_ref[...]).astype(BF16)
    acc = x
    for c in range(D_FF // D_MODEL):
        cols = slice(c * D_MODEL, (c + 1) * D_MODEL)
        hid = jnp.dot(h, w1_ref[:, cols], preferred_element_type=F32)
        act = jnp.square(jnp.maximum(hid, 0.0)).astype(BF16)
        acc = acc + jnp.dot(act, w2_ref[cols, :], preferred_element_type=F32)
    out_ref[...] = acc


def _mlp(x, g, w1, w2):
    n = x.shape[0]
    tm = TM_PROJ
    tok = lambda i: (i, 0)
    const = lambda i: (0, 0)
    return pl.pallas_call(
        _mlp_kernel,
        grid=(n // tm,),
        in_specs=[pl.BlockSpec((tm, D_MODEL), tok),
                  pl.BlockSpec((1, D_MODEL), const),
                  _resident((D_MODEL, D_FF), const),
                  _resident((D_FF, D_MODEL), const)],
        out_specs=pl.BlockSpec((tm, D_MODEL), tok),
        out_shape=jax.ShapeDtypeStruct((n, D_MODEL), F32),
        compiler_params=_params("parallel"),
        name="sqrelu_mlp",
    )(x, g, w1, w2)


def kernel(x, positions, norm_mix, norm_mlp, attn_w_qkv, attn_w_o, attn_q_gain, attn_k_gain,
           attn_lam_q1, attn_lam_k1, attn_lam_q2, attn_lam_k2, attn_sub_gain,
           pool_w, pool_scale, mlp_w1, mlp_w2):
    batch, seq, d = x.shape
    assert d == D_MODEL and seq % TM_PROJ == 0 and seq % TQ_ATTN == 0
    n = batch * seq
    xf = x.reshape(n, d)

    cos, sin = _rope_tables(positions)
    gsum, swap = _head_helper_matrices()
    maps_per_head = LANES // QK_DIM
    scale = 1.0 / math.sqrt(QK_DIM)
    w_qkv = attn_w_qkv.astype(BF16)
    w_o = attn_w_o.astype(BF16)
    w_pool = pool_w.astype(BF16)
    w1 = mlp_w1.astype(BF16)
    w2 = mlp_w2.astype(BF16)

    for i in range(DEPTH):
        j = i // N_MIXERS
        g_mix = norm_mix[i].reshape(1, d)
        if i % N_MIXERS == 0:
            q_gain = jnp.tile(attn_q_gain[j] * scale, maps_per_head).reshape(1, LANES)
            k_gain = jnp.tile(attn_k_gain[j], maps_per_head).reshape(1, LANES)
            q, k, v = _qkv_proj(xf, g_mix, w_qkv[j], q_gain, k_gain, cos, sin, gsum, swap)
            lam_vecs = jnp.stack([attn_lam_q1[j], attn_lam_k1[j], attn_lam_q2[j], attn_lam_k2[j]])
            o = _attention(lam_vecs, q, k, v, attn_sub_gain[j].reshape(1, V_DIM),
                           _lambda_init(i), batch, seq)
            xf = _out_proj(xf, o, w_o[j])
        else:
            xf = _pool_mixer(xf, g_mix, w_pool[j], pool_scale[j].reshape(1, d), seq)
        xf = _mlp(xf, norm_mlp[i].reshape(1, d), w1[i], w2[i])
    return xf.reshape(batch, seq, d)
```

```python
import functools
import math

import numpy as np
import jax
import jax.numpy as jnp
from jax import lax
from jax.experimental import pallas as pl
from jax.experimental.pallas import tpu as pltpu

D_MODEL = 1024
DEPTH = 4
N_MIXERS = 2
QK_DIM = 64
V_DIM = 2 * QK_DIM
N_HEADS = D_MODEL // V_DIM
ROPE_DIM = QK_DIM // 4
ROPE_THETA = 500000.0
POOL_WINDOWS = (2, 4, 8, 16)
POOL_GROUP = D_MODEL // len(POOL_WINDOWS)
D_FF = 4 * D_MODEL
EPS = 1e-6

LANES = 128
POOL_HALO = 8
ONES_ROWS = 16
_NT_DIMS = (((1,), (1,)), ((), ()))
VMEM_LIMIT = 48 * 1024 * 1024

TM_PROJ = 512
TQ_ATTN = 256
ATTN_TILES_PER_STEP = 8
TM_ROPE = 2048

F32 = jnp.float32
BF16 = jnp.bfloat16


def _lambda_init(layer_idx):
    return 0.8 - 0.6 * math.exp(-0.3 * layer_idx)


def _rms(x, g):
    ms = jnp.mean(x * x, axis=-1, keepdims=True)
    return x * lax.rsqrt(ms + EPS) * g


def _params(*sem):
    return pltpu.CompilerParams(dimension_semantics=sem, vmem_limit_bytes=VMEM_LIMIT)


def _resident(block_shape, index_map):
    return pl.BlockSpec(block_shape, index_map, pipeline_mode=pl.Buffered(1))


def _rope_kernel(pos_ref, invf_ref, cos_ref, sin_ref):
    ang = pos_ref[...].astype(F32) * invf_ref[...]
    cos_ref[...] = jnp.cos(ang)
    sin_ref[...] = jnp.sin(ang)


def _rope_tables(positions):
    n = positions.size
    inv_freq = ROPE_THETA ** (-jnp.arange(0, ROPE_DIM, 2, dtype=F32) / ROPE_DIM)
    half = ROPE_DIM // 2
    per_map = jnp.concatenate([inv_freq, inv_freq, jnp.zeros((QK_DIM - 2 * half,), F32)])
    invf = jnp.tile(per_map, LANES // QK_DIM).reshape(1, LANES)
    pos = positions.reshape(n, 1)
    out = jax.ShapeDtypeStruct((n, LANES), F32)
    return pl.pallas_call(
        _rope_kernel,
        grid=(n // TM_ROPE,),
        in_specs=[pl.BlockSpec((TM_ROPE, 1), lambda i: (i, 0)),
                  pl.BlockSpec((1, LANES), lambda i: (0, 0))],
        out_specs=[pl.BlockSpec((TM_ROPE, LANES), lambda i: (i, 0))] * 2,
        out_shape=[out, out],
        compiler_params=_params("parallel"),
        name="rope_tables",
    )(pos, invf)


def _head_helper_matrices():
    lane = np.arange(LANES)
    gsum = (lane[:, None] // QK_DIM == lane[None, :] // QK_DIM).astype(np.float32)
    half = ROPE_DIM // 2
    swap = np.zeros((LANES, LANES), np.float32)
    for j in range(LANES):
        d = j % QK_DIM
        if d < half:
            swap[j + half, j] = -1.0
        elif d < ROPE_DIM:
            swap[j - half, j] = 1.0
    return jnp.asarray(gsum, BF16), jnp.asarray(swap, BF16)


def _qkv_kernel(x_ref, g_ref, wqk_ref, wvt_ref, qg_ref, kg_ref, cos_ref, sin_ref, gsum_ref, swap_ref,
                q_ref, k_ref, vt_ref):
    h = _rms(x_ref[...], g_ref[...]).astype(BF16)
    cos = cos_ref[...]
    sin = sin_ref[...]
    gsum = gsum_ref[...]
    swap = swap_ref[...]
    for sec, (out_ref, gain_ref) in enumerate(((q_ref, qg_ref), (k_ref, kg_ref))):
        y = jnp.dot(h, wqk_ref[:, sec * D_MODEL:(sec + 1) * D_MODEL], preferred_element_type=F32)
        gain = gain_ref[...]
        for hd in range(N_HEADS):
            cols = slice(hd * LANES, (hd + 1) * LANES)
            yh = y[:, cols]
            ss = jnp.dot((yh * yh).astype(BF16), gsum, preferred_element_type=F32)
            inv = lax.rsqrt(ss * (1.0 / QK_DIM) + EPS)
            yg = yh * gain
            sw = jnp.dot(yg.astype(BF16), swap, preferred_element_type=F32)
            out_ref[:, cols] = ((yg * inv) * cos + (sw * inv) * sin).astype(BF16)
    vt = lax.dot_general(wvt_ref[...], h, _NT_DIMS, preferred_element_type=F32)
    vt_ref[...] = vt.astype(BF16)


def _qkv_proj(x, g, w_qk, w_vt, q_gain, k_gain, cos, sin, gsum, swap):
    n = x.shape[0]
    tm = TM_PROJ
    tok = lambda i: (i, 0)
    const = lambda i: (0, 0)
    out = jax.ShapeDtypeStruct((n, D_MODEL), BF16)
    return pl.pallas_call(
        _qkv_kernel,
        grid=(n // tm,),
        in_specs=[pl.BlockSpec((tm, D_MODEL), tok),
                  pl.BlockSpec((1, D_MODEL), const),
                  _resident((D_MODEL, 2 * D_MODEL), const),
                  _resident((D_MODEL, D_MODEL), const),
                  pl.BlockSpec((1, LANES), const),
                  pl.BlockSpec((1, LANES), const),
                  pl.BlockSpec((tm, LANES), tok),
                  pl.BlockSpec((tm, LANES), tok),
                  pl.BlockSpec((LANES, LANES), const),
                  pl.BlockSpec((LANES, LANES), const)],
        out_specs=[pl.BlockSpec((tm, D_MODEL), tok),
                   pl.BlockSpec((tm, D_MODEL), tok),
                   pl.BlockSpec((D_MODEL, tm), lambda i: (0, i))],
        out_shape=[out, out, jax.ShapeDtypeStruct((D_MODEL, n), BF16)],
        compiler_params=_params("parallel"),
        name="qkv_proj",
    )(x, g, w_qk, w_vt, q_gain, k_gain, cos, sin, gsum, swap)


def _attn_kernel(lam_ref, q_ref, k_ref, vt_ref, sg_ref, o_ref, vext_ref, *, lambda_init, tq):
    seq = vt_ref.shape[1]

    @pl.when(pl.program_id(2) == 0)
    def _():
        vext_ref[0:V_DIM, :] = vt_ref[...]
        row = lax.broadcasted_iota(jnp.int32, (ONES_ROWS, seq), 0)
        vext_ref[V_DIM:, :] = jnp.where(row == 0, 1.0, 0.0).astype(BF16)

    lv = lam_ref[...]
    lam = (jnp.exp(jnp.sum(lv[0:1] * lv[1:2], axis=-1, keepdims=True))
           - jnp.exp(jnp.sum(lv[2:3] * lv[3:4], axis=-1, keepdims=True)) + lambda_init)
    sg = sg_ref[...]
    k = k_ref[...]

    def scores(t):
        q = q_ref[t * tq:(t + 1) * tq, :]
        lane = lax.broadcasted_iota(jnp.int32, q.shape, 1)
        zero = jnp.zeros_like(q)
        q_both = jnp.concatenate([jnp.where(lane < QK_DIM, q, zero),
                                  jnp.where(lane >= QK_DIM, q, zero)], axis=0)
        return lax.dot_general(k, q_both, _NT_DIMS, preferred_element_type=F32)

    def finish(t, st):
        m = jnp.max(st, axis=0, keepdims=True)
        p = jnp.exp2(st - m).astype(BF16)
        acc = jnp.dot(vext_ref[...], p, preferred_element_type=F32)
        l1 = acc[V_DIM:V_DIM + 1, 0:tq]
        l2 = acc[V_DIM:V_DIM + 1, tq:2 * tq]
        ot = acc[0:V_DIM, 0:tq] * (1.0 / l1) - acc[0:V_DIM, tq:2 * tq] * (lam / l2)
        ms = jnp.mean(ot * ot, axis=0, keepdims=True)
        ot = ot * lax.rsqrt(ms + EPS) * sg * (1.0 - lambda_init)
        o_ref[t * tq:(t + 1) * tq, :] = ot.T.astype(BF16)

    n_tiles = q_ref.shape[0] // tq
    st_next = scores(0)
    for t in range(n_tiles):
        st = st_next
        if t + 1 < n_tiles:
            st_next = scores(t + 1)
        finish(t, st)


def _attention(lam_vecs, q, k, vt, sub_gain, lambda_init, batch, seq):
    tq = TQ_ATTN
    rows = tq * ATTN_TILES_PER_STEP
    n_q = seq // rows
    q_map = lambda b, h, i: (b * n_q + i, h)
    const = lambda b, h, i: (0, 0)
    return pl.pallas_call(
        functools.partial(_attn_kernel, lambda_init=lambda_init, tq=tq),
        grid=(batch, N_HEADS, n_q),
        in_specs=[pl.BlockSpec((4, QK_DIM), const),
                  pl.BlockSpec((rows, LANES), q_map),
                  pl.BlockSpec((seq, LANES), lambda b, h, i: (b, h)),
                  pl.BlockSpec((V_DIM, seq), lambda b, h, i: (h, b)),
                  pl.BlockSpec((V_DIM, 1), const)],
        out_specs=pl.BlockSpec((rows, LANES), q_map),
        out_shape=jax.ShapeDtypeStruct((batch * seq, D_MODEL), BF16),
        scratch_shapes=[pltpu.VMEM((V_DIM + ONES_ROWS, seq), BF16)],
        compiler_params=_params("parallel", "parallel", "arbitrary"),
        name="diff_attention",
    )(lam_vecs, q, k, vt, sub_gain)


def _oproj_kernel(x_ref, o_ref, w_ref, out_ref):
    out_ref[...] = x_ref[...] + jnp.dot(o_ref[...], w_ref[...], preferred_element_type=F32)


def _out_proj(x, o, w):
    n = x.shape[0]
    tm = TM_PROJ
    tok = lambda i: (i, 0)
    return pl.pallas_call(
        _oproj_kernel,
        grid=(n // tm,),
        in_specs=[pl.BlockSpec((tm, D_MODEL), tok),
                  pl.BlockSpec((tm, D_MODEL), tok),
                  _resident((D_MODEL, D_MODEL), lambda i: (0, 0))],
        out_specs=pl.BlockSpec((tm, D_MODEL), tok),
        out_shape=jax.ShapeDtypeStruct((n, D_MODEL), F32),
        compiler_params=_params("parallel"),
        name="attn_out_proj",
    )(x, o, w)


def _pool_kernel(xp_ref, x_ref, xn_ref, g_ref, w_ref, sc_ref, out_ref, hh_ref, *, seq, tm):
    tiles_per_seq = seq // tm
    j = pl.program_id(0) % tiles_per_seq
    g = g_ref[...]
    x = x_ref[...]
    h = _rms(x, g)
    hp = jnp.where(j > 0, _rms(xp_ref[...], g), 0.0)
    hn = jnp.where(j < tiles_per_seq - 1, _rms(xn_ref[...], g), 0.0)
    hh_ref[0:POOL_HALO, :] = hp
    hh_ref[POOL_HALO:POOL_HALO + tm, :] = h
    hh_ref[POOL_HALO + tm:, :] = hn
    t = j * tm + lax.broadcasted_iota(jnp.int32, (tm, 1), 0)
    sc = sc_ref[...]
    for gi, win in enumerate(POOL_WINDOWS):
        r = win // 2
        cols = slice(gi * POOL_GROUP, (gi + 1) * POOL_GROUP)
        tot = hh_ref[pl.ds(POOL_HALO - r, tm), cols]
        for off in range(-r + 1, r):
            tot = tot + hh_ref[pl.ds(POOL_HALO + off, tm), cols]
        cnt = (jnp.minimum(t + r, seq) - jnp.maximum(t - r, 0)).astype(F32)
        diff = (tot / cnt - h[:, cols]).astype(BF16)
        y = jnp.dot(diff, w_ref[gi], preferred_element_type=F32)
        out_ref[:, cols] = x[:, cols] + y * sc[:, cols]


def _pool_mixer(x, g, w, scale, seq):
    n = x.shape[0]
    tm = TM_PROJ
    per_tile = tm // POOL_HALO
    last_halo_block = n // POOL_HALO - 1
    tok = lambda i: (i, 0)
    const = lambda i: (0, 0)
    return pl.pallas_call(
        functools.partial(_pool_kernel, seq=seq, tm=tm),
        grid=(n // tm,),
        in_specs=[pl.BlockSpec((POOL_HALO, D_MODEL), lambda i: (jnp.maximum(i * per_tile - 1, 0), 0)),
                  pl.BlockSpec((tm, D_MODEL), tok),
                  pl.BlockSpec((POOL_HALO, D_MODEL),
                               lambda i: (jnp.minimum((i + 1) * per_tile, last_halo_block), 0)),
                  pl.BlockSpec((1, D_MODEL), const),
                  _resident((len(POOL_WINDOWS), POOL_GROUP, POOL_GROUP), lambda i: (0, 0, 0)),
                  pl.BlockSpec((1, D_MODEL), const)],
        out_specs=pl.BlockSpec((tm, D_MODEL), tok),
        out_shape=jax.ShapeDtypeStruct((n, D_MODEL), F32),
        scratch_shapes=[pltpu.VMEM((tm + 2 * POOL_HALO, D_MODEL), F32)],
        compiler_params=_params("parallel"),
        name="pool_mixer",
    )(x, x, x, g, w, scale)


def _mlp_kernel(x_ref, g_ref, w1_ref, w2_ref, out_ref):
    x = x_ref[...]
    h = _rms(x, g_ref[...]).astype(BF16)
    acc = x
    for c in range(D_FF // D_MODEL):
        cols = slice(c * D_MODEL, (c + 1) * D_MODEL)
        hid = jnp.dot(h, w1_ref[:, cols], preferred_element_type=F32)
        act = jnp.square(jnp.maximum(hid, 0.0)).astype(BF16)
        acc = acc + jnp.dot(act, w2_ref[cols, :], preferred_element_type=F32)
    out_ref[...] = acc


def _mlp(x, g, w1, w2):
    n = x.shape[0]
    tm = TM_PROJ
    tok = lambda i: (i, 0)
    const = lambda i: (0, 0)
    return pl.pallas_call(
        _mlp_kernel,
        grid=(n // tm,),
        in_specs=[pl.BlockSpec((tm, D_MODEL), tok),
                  pl.BlockSpec((1, D_MODEL), const),
                  _resident((D_MODEL, D_FF), const),
                  _resident((D_FF, D_MODEL), const)],
        out_specs=pl.BlockSpec((tm, D_MODEL), tok),
        out_shape=jax.ShapeDtypeStruct((n, D_MODEL), F32),
        compiler_params=_params("parallel"),
        name="sqrelu_mlp",
    )(x, g, w1, w2)


def kernel(x, positions, norm_mix, norm_mlp, attn_w_qkv, attn_w_o, attn_q_gain, attn_k_gain,
           attn_lam_q1, attn_lam_k1, attn_lam_q2, attn_lam_k2, attn_sub_gain,
           pool_w, pool_scale, mlp_w1, mlp_w2):
    batch, seq, d = x.shape
    assert d == D_MODEL and seq % TM_PROJ == 0 and seq % (TQ_ATTN * ATTN_TILES_PER_STEP) == 0
    n = batch * seq
    xf = x.reshape(n, d)

    cos, sin = _rope_tables(positions)
    gsum, swap = _head_helper_matrices()
    maps_per_head = LANES // QK_DIM
    scale = math.log2(math.e) / math.sqrt(QK_DIM)
    w_qk = attn_w_qkv[:, :, :2 * D_MODEL].astype(BF16)
    w_vt = jnp.swapaxes(attn_w_qkv[:, :, 2 * D_MODEL:], 1, 2).astype(BF16)
    w_o = attn_w_o.astype(BF16)
    w_pool = pool_w.astype(BF16)
    w1 = mlp_w1.astype(BF16)
    w2 = mlp_w2.astype(BF16)

    for i in range(DEPTH):
        j = i // N_MIXERS
        g_mix = norm_mix[i].reshape(1, d)
        if i % N_MIXERS == 0:
            q_gain = jnp.tile(attn_q_gain[j] * scale, maps_per_head).reshape(1, LANES)
            k_gain = jnp.tile(attn_k_gain[j], maps_per_head).reshape(1, LANES)
            q, k, vt = _qkv_proj(xf, g_mix, w_qk[j], w_vt[j], q_gain, k_gain, cos, sin, gsum, swap)
            lam_vecs = jnp.stack([attn_lam_q1[j], attn_lam_k1[j], attn_lam_q2[j], attn_lam_k2[j]])
            o = _attention(lam_vecs, q, k, vt, attn_sub_gain[j].reshape(V_DIM, 1),
                           _lambda_init(i), batch, seq)
            xf = _out_proj(xf, o, w_o[j])
        else:
            xf = _pool_mixer(xf, g_mix, w_pool[j], pool_scale[j].reshape(1, d), seq)
        xf = _mlp(xf, norm_mlp[i].reshape(1, d), w1[i], w2[i])
    return xf.reshape(batch, seq, d)
```

```python
import functools
import math

import numpy as np
import jax
import jax.numpy as jnp
from jax import lax
from jax.experimental import pallas as pl
from jax.experimental.pallas import tpu as pltpu

D_MODEL = 1024
DEPTH = 4
N_MIXERS = 2
QK_DIM = 64
V_DIM = 2 * QK_DIM
N_HEADS = D_MODEL // V_DIM
ROPE_DIM = QK_DIM // 4
ROPE_THETA = 500000.0
POOL_WINDOWS = (2, 4, 8, 16)
POOL_GROUP = D_MODEL // len(POOL_WINDOWS)
D_FF = 4 * D_MODEL
EPS = 1e-6

LANES = 128
POOL_HALO = 8
FAST_SCORE_BOUND = 60.0
_NT_DIMS = (((1,), (1,)), ((), ()))
VMEM_LIMIT = 48 * 1024 * 1024

TM_PROJ = 512
TQ_ATTN = 256
ATTN_TILES_PER_STEP = 8
TM_ROPE = 2048

F32 = jnp.float32
BF16 = jnp.bfloat16


def _lambda_init(layer_idx):
    return 0.8 - 0.6 * math.exp(-0.3 * layer_idx)


def _rms(x, g):
    ms = jnp.mean(x * x, axis=-1, keepdims=True)
    return x * lax.rsqrt(ms + EPS) * g


def _params(*sem):
    return pltpu.CompilerParams(dimension_semantics=sem, vmem_limit_bytes=VMEM_LIMIT)


def _resident(block_shape, index_map):
    return pl.BlockSpec(block_shape, index_map, pipeline_mode=pl.Buffered(1))


def _rope_kernel(pos_ref, invf_ref, cos_ref, sin_ref):
    ang = pos_ref[...].astype(F32) * invf_ref[...]
    cos_ref[...] = jnp.cos(ang)
    sin_ref[...] = jnp.sin(ang)


def _rope_tables(positions):
    n = positions.size
    inv_freq = ROPE_THETA ** (-jnp.arange(0, ROPE_DIM, 2, dtype=F32) / ROPE_DIM)
    half = ROPE_DIM // 2
    per_map = jnp.concatenate([inv_freq, inv_freq, jnp.zeros((QK_DIM - 2 * half,), F32)])
    invf = jnp.tile(per_map, LANES // QK_DIM).reshape(1, LANES)
    pos = positions.reshape(n, 1)
    out = jax.ShapeDtypeStruct((n, LANES), F32)
    return pl.pallas_call(
        _rope_kernel,
        grid=(n // TM_ROPE,),
        in_specs=[pl.BlockSpec((TM_ROPE, 1), lambda i: (i, 0)),
                  pl.BlockSpec((1, LANES), lambda i: (0, 0))],
        out_specs=[pl.BlockSpec((TM_ROPE, LANES), lambda i: (i, 0))] * 2,
        out_shape=[out, out],
        compiler_params=_params("parallel"),
        name="rope_tables",
    )(pos, invf)


def _head_helper_matrix():
    lane = np.arange(LANES)
    gsum = (lane[:, None] // QK_DIM == lane[None, :] // QK_DIM).astype(np.float32)
    half = ROPE_DIM // 2
    swap = np.zeros((LANES, LANES), np.float32)
    for j in range(LANES):
        d = j % QK_DIM
        if d < half:
            swap[j + half, j] = -1.0
        elif d < ROPE_DIM:
            swap[j - half, j] = 1.0
    both = np.zeros((2 * LANES, 2 * LANES), np.float32)
    both[:LANES, :LANES] = gsum
    both[LANES:, LANES:] = swap
    return jnp.asarray(both, BF16)


def _qkv_kernel(x_ref, g_ref, wqk_ref, wvt_ref, qg_ref, kg_ref, cos_ref, sin_ref, helper_ref,
                q_ref, k_ref, vt_ref):
    h = _rms(x_ref[...], g_ref[...]).astype(BF16)
    cos = cos_ref[...]
    sin = sin_ref[...]
    helper = helper_ref[...]
    for sec, (out_ref, gain_ref) in enumerate(((q_ref, qg_ref), (k_ref, kg_ref))):
        y = jnp.dot(h, wqk_ref[:, sec * D_MODEL:(sec + 1) * D_MODEL], preferred_element_type=F32)
        gain = gain_ref[...]
        for hd in range(N_HEADS):
            cols = slice(hd * LANES, (hd + 1) * LANES)
            yh = y[:, cols]
            yg = yh * gain
            lhs = jnp.concatenate([(yh * yh).astype(BF16), yg.astype(BF16)], axis=1)
            both = jnp.dot(lhs, helper, preferred_element_type=F32)
            inv = lax.rsqrt(both[:, :LANES] * (1.0 / QK_DIM) + EPS)
            out_ref[:, cols] = ((yg * inv) * cos + (both[:, LANES:] * inv) * sin).astype(BF16)
    vt = lax.dot_general(wvt_ref[...], h, _NT_DIMS, preferred_element_type=F32)
    vt_ref[...] = vt.astype(BF16)


def _qkv_proj(x, g, w_qk, w_vt, q_gain, k_gain, cos, sin, helper):
    n = x.shape[0]
    tm = TM_PROJ
    tok = lambda i: (i, 0)
    const = lambda i: (0, 0)
    out = jax.ShapeDtypeStruct((n, D_MODEL), BF16)
    return pl.pallas_call(
        _qkv_kernel,
        grid=(n // tm,),
        in_specs=[pl.BlockSpec((tm, D_MODEL), tok),
                  pl.BlockSpec((1, D_MODEL), const),
                  _resident((D_MODEL, 2 * D_MODEL), const),
                  _resident((D_MODEL, D_MODEL), const),
                  pl.BlockSpec((1, LANES), const),
                  pl.BlockSpec((1, LANES), const),
                  pl.BlockSpec((tm, LANES), tok),
                  pl.BlockSpec((tm, LANES), tok),
                  pl.BlockSpec((2 * LANES, 2 * LANES), const)],
        out_specs=[pl.BlockSpec((tm, D_MODEL), tok),
                   pl.BlockSpec((tm, D_MODEL), tok),
                   pl.BlockSpec((D_MODEL, tm), lambda i: (0, i))],
        out_shape=[out, out, jax.ShapeDtypeStruct((D_MODEL, n), BF16)],
        compiler_params=_params("parallel"),
        name="qkv_proj",
    )(x, g, w_qk, w_vt, q_gain, k_gain, cos, sin, helper)


def _attn_kernel(lam_ref, q_ref, k_ref, vt_ref, sg_ref, o_ref, *, lambda_init, tq, subtract_max):
    lv = lam_ref[...]
    lam = (jnp.exp(jnp.sum(lv[0:1] * lv[1:2], axis=-1, keepdims=True))
           - jnp.exp(jnp.sum(lv[2:3] * lv[3:4], axis=-1, keepdims=True)) + lambda_init)
    sg = sg_ref[...]
    k = k_ref[...]
    vt = vt_ref[...]

    def probabilities(t):
        q = q_ref[t * tq:(t + 1) * tq, :]
        lane = lax.broadcasted_iota(jnp.int32, q.shape, 1)
        zero = jnp.zeros_like(q)
        q_both = jnp.concatenate([jnp.where(lane < QK_DIM, q, zero),
                                  jnp.where(lane >= QK_DIM, q, zero)], axis=0)
        st = lax.dot_general(k, q_both, _NT_DIMS, preferred_element_type=F32)
        if subtract_max:
            st = st - jnp.max(st, axis=0, keepdims=True)
        p = jnp.exp2(st)
        return p.astype(BF16), jnp.sum(p, axis=0, keepdims=True)

    def finish(t, p, l):
        acc = jnp.dot(vt, p, preferred_element_type=F32)
        c1 = 1.0 / l[:, 0:tq]
        c2 = lam / l[:, tq:2 * tq]
        ot = acc[:, 0:tq] * c1 - acc[:, tq:2 * tq] * c2
        ms = jnp.mean(ot * ot, axis=0, keepdims=True)
        ot = ot * lax.rsqrt(ms + EPS) * sg * (1.0 - lambda_init)
        o_ref[t * tq:(t + 1) * tq, :] = ot.T.astype(BF16)

    n_tiles = q_ref.shape[0] // tq
    nxt = probabilities(0)
    for t in range(n_tiles):
        cur = nxt
        if t + 1 < n_tiles:
            nxt = probabilities(t + 1)
        finish(t, *cur)


def _score_bound(q_gain, k_gain):
    return QK_DIM * jnp.max(jnp.abs(q_gain)) * jnp.max(jnp.abs(k_gain))


def _attention(lam_vecs, q, k, vt, sub_gain, lambda_init, batch, seq, subtract_max):
    tq = TQ_ATTN
    rows = tq * ATTN_TILES_PER_STEP
    n_q = seq // rows
    q_map = lambda b, h, i: (b * n_q + i, h)
    const = lambda b, h, i: (0, 0)
    return pl.pallas_call(
        functools.partial(_attn_kernel, lambda_init=lambda_init, tq=tq, subtract_max=subtract_max),
        grid=(batch, N_HEADS, n_q),
        in_specs=[pl.BlockSpec((4, QK_DIM), const),
                  pl.BlockSpec((rows, LANES), q_map),
                  pl.BlockSpec((seq, LANES), lambda b, h, i: (b, h)),
                  pl.BlockSpec((V_DIM, seq), lambda b, h, i: (h, b)),
                  pl.BlockSpec((V_DIM, 1), const)],
        out_specs=pl.BlockSpec((rows, LANES), q_map),
        out_shape=jax.ShapeDtypeStruct((batch * seq, D_MODEL), BF16),
        compiler_params=_params("parallel", "parallel", "parallel"),
        name="diff_attention" if subtract_max else "diff_attention_bounded",
    )(lam_vecs, q, k, vt, sub_gain)


def _pool_kernel(xp_ref, x_ref, xn_ref, g_ref, w_ref, sc_ref, out_ref, hh_ref, *, seq, tm):
    tiles_per_seq = seq // tm
    j = pl.program_id(0) % tiles_per_seq
    g = g_ref[...]
    x = x_ref[...]
    h = _rms(x, g)
    hp = jnp.where(j > 0, _rms(xp_ref[...], g), 0.0)
    hn = jnp.where(j < tiles_per_seq - 1, _rms(xn_ref[...], g), 0.0)
    hh_ref[0:POOL_HALO, :] = hp
    hh_ref[POOL_HALO:POOL_HALO + tm, :] = h
    hh_ref[POOL_HALO + tm:, :] = hn
    t = j * tm + lax.broadcasted_iota(jnp.int32, (tm, 1), 0)
    sc = sc_ref[...]
    for gi, win in enumerate(POOL_WINDOWS):
        r = win // 2
        cols = slice(gi * POOL_GROUP, (gi + 1) * POOL_GROUP)
        tot = hh_ref[pl.ds(POOL_HALO - r, tm), cols]
        for off in range(-r + 1, r):
            tot = tot + hh_ref[pl.ds(POOL_HALO + off, tm), cols]
        cnt = (jnp.minimum(t + r, seq) - jnp.maximum(t - r, 0)).astype(F32)
        diff = (tot / cnt - h[:, cols]).astype(BF16)
        y = jnp.dot(diff, w_ref[gi], preferred_element_type=F32)
        out_ref[:, cols] = x[:, cols] + y * sc[:, cols]


def _pool_mixer(x, g, w, scale, seq):
    n = x.shape[0]
    tm = TM_PROJ
    per_tile = tm // POOL_HALO
    last_halo_block = n // POOL_HALO - 1
    tok = lambda i: (i, 0)
    const = lambda i: (0, 0)
    return pl.pallas_call(
        functools.partial(_pool_kernel, seq=seq, tm=tm),
        grid=(n // tm,),
        in_specs=[pl.BlockSpec((POOL_HALO, D_MODEL), lambda i: (jnp.maximum(i * per_tile - 1, 0), 0)),
                  pl.BlockSpec((tm, D_MODEL), tok),
                  pl.BlockSpec((POOL_HALO, D_MODEL),
                               lambda i: (jnp.minimum((i + 1) * per_tile, last_halo_block), 0)),
                  pl.BlockSpec((1, D_MODEL), const),
                  _resident((len(POOL_WINDOWS), POOL_GROUP, POOL_GROUP), lambda i: (0, 0, 0)),
                  pl.BlockSpec((1, D_MODEL), const)],
        out_specs=pl.BlockSpec((tm, D_MODEL), tok),
        out_shape=jax.ShapeDtypeStruct((n, D_MODEL), F32),
        scratch_shapes=[pltpu.VMEM((tm + 2 * POOL_HALO, D_MODEL), F32)],
        compiler_params=_params("parallel"),
        name="pool_mixer",
    )(x, x, x, g, w, scale)


def _mlp_kernel(*refs, with_out_proj):
    if with_out_proj:
        x_ref, o_ref, wo_ref, g_ref, w1_ref, w2_ref, out_ref = refs
        x = x_ref[...] + jnp.dot(o_ref[...], wo_ref[...], preferred_element_type=F32)
    else:
        x_ref, g_ref, w1_ref, w2_ref, out_ref = refs
        x = x_ref[...]
    h = _rms(x, g_ref[...]).astype(BF16)
    acc = x
    for c in range(D_FF // D_MODEL):
        cols = slice(c * D_MODEL, (c + 1) * D_MODEL)
        hid = jnp.dot(h, w1_ref[:, cols], preferred_element_type=F32)
        act = jnp.square(jnp.maximum(hid, 0.0)).astype(BF16)
        acc = acc + jnp.dot(act, w2_ref[cols, :], preferred_element_type=F32)
    out_ref[...] = acc


def _mlp(x, g, w1, w2, attn_out=None, w_o=None):
    n = x.shape[0]
    tm = TM_PROJ
    tok = lambda i: (i, 0)
    const = lambda i: (0, 0)
    with_out_proj = attn_out is not None
    operands = [x]
    in_specs = [pl.BlockSpec((tm, D_MODEL), tok)]
    if with_out_proj:
        operands += [attn_out, w_o]
        in_specs += [pl.BlockSpec((tm, D_MODEL), tok), _resident((D_MODEL, D_MODEL), const)]
    operands += [g, w1, w2]
    in_specs += [pl.BlockSpec((1, D_MODEL), const),
                 _resident((D_MODEL, D_FF), const),
                 _resident((D_FF, D_MODEL), const)]
    return pl.pallas_call(
        functools.partial(_mlp_kernel, with_out_proj=with_out_proj),
        grid=(n // tm,),
        in_specs=in_specs,
        out_specs=pl.BlockSpec((tm, D_MODEL), tok),
        out_shape=jax.ShapeDtypeStruct((n, D_MODEL), F32),
        compiler_params=_params("parallel"),
        name="out_proj_mlp" if with_out_proj else "sqrelu_mlp",
    )(*operands)


def kernel(x, positions, norm_mix, norm_mlp, attn_w_qkv, attn_w_o, attn_q_gain, attn_k_gain,
           attn_lam_q1, attn_lam_k1, attn_lam_q2, attn_lam_k2, attn_sub_gain,
           pool_w, pool_scale, mlp_w1, mlp_w2):
    batch, seq, d = x.shape
    assert d == D_MODEL and seq % TM_PROJ == 0 and seq % (TQ_ATTN * ATTN_TILES_PER_STEP) == 0
    n = batch * seq
    xf = x.reshape(n, d)

    cos, sin = _rope_tables(positions)
    helper = _head_helper_matrix()
    maps_per_head = LANES // QK_DIM
    scale = math.log2(math.e) / math.sqrt(QK_DIM)
    w_qk = attn_w_qkv[:, :, :2 * D_MODEL].astype(BF16)
    w_vt = jnp.swapaxes(attn_w_qkv[:, :, 2 * D_MODEL:], 1, 2).astype(BF16)
    w_o = attn_w_o.astype(BF16)
    w_pool = pool_w.astype(BF16)
    w1 = mlp_w1.astype(BF16)
    w2 = mlp_w2.astype(BF16)

    for i in range(DEPTH):
        j = i // N_MIXERS
        g_mix = norm_mix[i].reshape(1, d)
        if i % N_MIXERS == 0:
            q_gain = jnp.tile(attn_q_gain[j] * scale, maps_per_head).reshape(1, LANES)
            k_gain = jnp.tile(attn_k_gain[j], maps_per_head).reshape(1, LANES)
            q, k, vt = _qkv_proj(xf, g_mix, w_qk[j], w_vt[j], q_gain, k_gain, cos, sin, helper)
            lam_vecs = jnp.stack([attn_lam_q1[j], attn_lam_k1[j], attn_lam_q2[j], attn_lam_k2[j]])
            attend = functools.partial(_attention, lambda_init=_lambda_init(i), batch=batch, seq=seq)
            o = lax.cond(_score_bound(q_gain, k_gain) <= FAST_SCORE_BOUND,
                         functools.partial(attend, subtract_max=False),
                         functools.partial(attend, subtract_max=True),
                         lam_vecs, q, k, vt, attn_sub_gain[j].reshape(V_DIM, 1))
            xf = _mlp(xf, norm_mlp[i].reshape(1, d), w1[i], w2[i], attn_out=o, w_o=w_o[j])
        else:
            xf = _pool_mixer(xf, g_mix, w_pool[j], pool_scale[j].reshape(1, d), seq)
            xf = _mlp(xf, norm_mlp[i].reshape(1, d), w1[i], w2[i])
    return xf.reshape(batch, seq, d)
```

```python
import functools
import math

import jax
import jax.numpy as jnp
from jax import lax
from jax.experimental import pallas as pl
from jax.experimental.pallas import tpu as pltpu

D_MODEL = 1024
DEPTH = 4
N_MIXERS = 2
QK_DIM = 64
V_DIM = 2 * QK_DIM
N_HEADS = D_MODEL // V_DIM
ROPE_DIM = QK_DIM // 4
ROPE_THETA = 500000.0
POOL_WINDOWS = (2, 4, 8, 16)
POOL_GROUP = D_MODEL // len(POOL_WINDOWS)
D_FF = 4 * D_MODEL
EPS = 1e-6

LANES = 128
POOL_HALO = 8
FAST_SCORE_BOUND = 60.0
_NT_DIMS = (((1,), (1,)), ((), ()))
VMEM_LIMIT = 48 * 1024 * 1024

TM_PROJ = 512
QKV_ROW_BLOCK = 512
TQ_ATTN = 256
ATTN_TILES_PER_STEP = 8
TN_ROPE = 2048

F32 = jnp.float32
BF16 = jnp.bfloat16


def _lambda_init(layer_idx):
    return 0.8 - 0.6 * math.exp(-0.3 * layer_idx)


def _rms(x, g):
    ms = jnp.mean(x * x, axis=-1, keepdims=True)
    return x * lax.rsqrt(ms + EPS) * g


def _params(*sem):
    return pltpu.CompilerParams(dimension_semantics=sem, vmem_limit_bytes=VMEM_LIMIT)


def _resident(block_shape, index_map):
    return pl.BlockSpec(block_shape, index_map, pipeline_mode=pl.Buffered(1))


def _rope_kernel(pos_ref, invf_ref, cos_ref, sin_ref):
    ang = pos_ref[...].astype(F32) * invf_ref[...]
    cos_ref[...] = jnp.cos(ang)
    sin_ref[...] = jnp.sin(ang)


def _rope_tables(positions):
    n = positions.size
    half = ROPE_DIM // 2
    inv_freq = ROPE_THETA ** (-jnp.arange(0, ROPE_DIM, 2, dtype=F32) / ROPE_DIM)
    out = jax.ShapeDtypeStruct((half, n), F32)
    return pl.pallas_call(
        _rope_kernel,
        grid=(n // TN_ROPE,),
        in_specs=[pl.BlockSpec((1, TN_ROPE), lambda i: (0, i)),
                  pl.BlockSpec((half, 1), lambda i: (0, 0))],
        out_specs=[pl.BlockSpec((half, TN_ROPE), lambda i: (0, i))] * 2,
        out_shape=[out, out],
        compiler_params=_params("parallel"),
        name="rope_tables",
    )(positions.reshape(1, n), inv_freq.reshape(half, 1))


def _norm_rope_map(y, gain, cos, sin):
    half = ROPE_DIM // 2
    inv = lax.rsqrt(jnp.mean(y * y, axis=0, keepdims=True) + EPS)
    n = y * inv * gain
    n1 = n[0:half]
    n2 = n[half:ROPE_DIM]
    return jnp.concatenate([n1 * cos - n2 * sin, n2 * cos + n1 * sin, n[ROPE_DIM:]], axis=0)


def _qkv_kernel(x_ref, g_ref, wt_ref, qg_ref, kg_ref, cos_ref, sin_ref, qt_ref, k_ref, vt_ref):
    h = _rms(x_ref[...], g_ref[...]).astype(BF16)
    cos = cos_ref[...]
    sin = sin_ref[...]
    qg = qg_ref[...]
    kg = kg_ref[...]

    def project(r0):
        return lax.dot_general(wt_ref[r0:r0 + QKV_ROW_BLOCK, :], h, _NT_DIMS,
                               preferred_element_type=F32)

    def heads(yt):
        for hd in range(QKV_ROW_BLOCK // V_DIM):
            yield hd * V_DIM, [yt[hd * V_DIM + m * QK_DIM:hd * V_DIM + (m + 1) * QK_DIM]
                               for m in range(V_DIM // QK_DIM)]

    for r0 in range(0, D_MODEL, QKV_ROW_BLOCK):
        for off, maps in heads(project(r0)):
            for m, y in enumerate(maps):
                rows = slice(r0 + off + m * QK_DIM, r0 + off + (m + 1) * QK_DIM)
                qt_ref[rows, :] = _norm_rope_map(y, qg, cos, sin).astype(BF16)
    for r0 in range(0, D_MODEL, QKV_ROW_BLOCK):
        for off, maps in heads(project(D_MODEL + r0)):
            kt = jnp.concatenate([_norm_rope_map(y, kg, cos, sin) for y in maps], axis=0)
            k_ref[:, r0 + off:r0 + off + V_DIM] = kt.T.astype(BF16)
    for r0 in range(0, D_MODEL, QKV_ROW_BLOCK):
        vt_ref[r0:r0 + QKV_ROW_BLOCK, :] = project(2 * D_MODEL + r0).astype(BF16)


def _qkv_proj(x, g, w_t, layer, q_gain, k_gain, cos, sin):
    n = x.shape[0]
    tm = TM_PROJ
    half = ROPE_DIM // 2
    tok = lambda i: (i, 0)
    tok_t = lambda i: (0, i)
    const = lambda i: (0, 0)
    feat_major = jax.ShapeDtypeStruct((D_MODEL, n), BF16)
    return pl.pallas_call(
        _qkv_kernel,
        grid=(n // tm,),
        in_specs=[pl.BlockSpec((tm, D_MODEL), tok),
                  pl.BlockSpec((1, D_MODEL), const),
                  _resident((None, 3 * D_MODEL, D_MODEL), lambda i: (layer, 0, 0)),
                  pl.BlockSpec((QK_DIM, 1), const),
                  pl.BlockSpec((QK_DIM, 1), const),
                  pl.BlockSpec((half, tm), tok_t),
                  pl.BlockSpec((half, tm), tok_t)],
        out_specs=[pl.BlockSpec((D_MODEL, tm), tok_t),
                   pl.BlockSpec((tm, D_MODEL), tok),
                   pl.BlockSpec((D_MODEL, tm), tok_t)],
        out_shape=[feat_major, jax.ShapeDtypeStruct((n, D_MODEL), BF16), feat_major],
        compiler_params=_params("parallel"),
        name="qkv_proj",
    )(x, g, w_t, q_gain, k_gain, cos, sin)


def _attn_kernel(lam_ref, qt_ref, k_ref, vt_ref, sg_ref, o_ref, *, lambda_init, tq, subtract_max):
    lv = lam_ref[...]
    lam = (jnp.exp(jnp.sum(lv[0:1] * lv[1:2], axis=-1, keepdims=True))
           - jnp.exp(jnp.sum(lv[2:3] * lv[3:4], axis=-1, keepdims=True)) + lambda_init)
    sg = sg_ref[...]
    k = k_ref[...]
    vt = vt_ref[...]

    def probabilities(t):
        qt = qt_ref[:, t * tq:(t + 1) * tq]
        zero = jnp.zeros((QK_DIM, tq), BF16)
        q_both = jnp.concatenate([jnp.concatenate([qt[0:QK_DIM], zero], axis=1),
                                  jnp.concatenate([zero, qt[QK_DIM:]], axis=1)], axis=0)
        st = jnp.dot(k, q_both, preferred_element_type=F32)
        if subtract_max:
            st = st - jnp.max(st, axis=0, keepdims=True)
        p = jnp.exp2(st)
        return p.astype(BF16), jnp.sum(p, axis=0, keepdims=True)

    def finish(t, p, l):
        acc = jnp.dot(vt, p, preferred_element_type=F32)
        c1 = 1.0 / l[:, 0:tq]
        c2 = lam / l[:, tq:2 * tq]
        ot = acc[:, 0:tq] * c1 - acc[:, tq:2 * tq] * c2
        ms = jnp.mean(ot * ot, axis=0, keepdims=True)
        ot = ot * lax.rsqrt(ms + EPS) * sg * (1.0 - lambda_init)
        o_ref[t * tq:(t + 1) * tq, :] = ot.T.astype(BF16)

    n_tiles = qt_ref.shape[1] // tq
    nxt = probabilities(0)
    for t in range(n_tiles):
        cur = nxt
        if t + 1 < n_tiles:
            nxt = probabilities(t + 1)
        finish(t, *cur)


def _score_bound(q_gain, k_gain):
    return QK_DIM * jnp.max(jnp.abs(q_gain)) * jnp.max(jnp.abs(k_gain))


def _attention(lam_vecs, qt, k, vt, sub_gain, lambda_init, batch, seq, subtract_max):
    tq = TQ_ATTN
    rows = tq * ATTN_TILES_PER_STEP
    n_q = seq // rows
    const = lambda b, h, i: (0, 0)
    return pl.pallas_call(
        functools.partial(_attn_kernel, lambda_init=lambda_init, tq=tq, subtract_max=subtract_max),
        grid=(batch, N_HEADS, n_q),
        in_specs=[pl.BlockSpec((4, QK_DIM), const),
                  pl.BlockSpec((V_DIM, rows), lambda b, h, i: (h, b * n_q + i)),
                  pl.BlockSpec((seq, LANES), lambda b, h, i: (b, h)),
                  pl.BlockSpec((V_DIM, seq), lambda b, h, i: (h, b)),
                  pl.BlockSpec((V_DIM, 1), const)],
        out_specs=pl.BlockSpec((rows, LANES), lambda b, h, i: (b * n_q + i, h)),
        out_shape=jax.ShapeDtypeStruct((batch * seq, D_MODEL), BF16),
        compiler_params=_params("parallel", "parallel", "parallel"),
        name="diff_attention" if subtract_max else "diff_attention_bounded",
    )(lam_vecs, qt, k, vt, sub_gain)


def _pool_kernel(xp_ref, x_ref, xn_ref, g_ref, w_ref, sc_ref, out_ref, *, seq, tm):
    tiles_per_seq = seq // tm
    j = pl.program_id(0) % tiles_per_seq
    g = g_ref[...]
    x = x_ref[...]
    h = _rms(x, g)
    hp = jnp.where(j > 0, _rms(xp_ref[...], g), 0.0)
    hn = jnp.where(j < tiles_per_seq - 1, _rms(xn_ref[...], g), 0.0)
    hh = jnp.concatenate([hp, h, hn], axis=0)
    n_rows = tm + 2 * POOL_HALO
    t = j * tm + lax.broadcasted_iota(jnp.int32, (tm, 1), 0)
    sc = sc_ref[...]
    for gi, win in enumerate(POOL_WINDOWS):
        r = win // 2
        cols = slice(gi * POOL_GROUP, (gi + 1) * POOL_GROUP)
        run = hh[:, cols]
        span = 1
        while span < win:
            run = run + pltpu.roll(run, span, 0)
            span *= 2
        if r > 1:
            run = pltpu.roll(run, n_rows - (r - 1), 0)
        tot = run[POOL_HALO:POOL_HALO + tm, :]
        cnt = (jnp.minimum(t + r, seq) - jnp.maximum(t - r, 0)).astype(F32)
        diff = (tot / cnt - h[:, cols]).astype(BF16)
        y = jnp.dot(diff, w_ref[gi], preferred_element_type=F32)
        out_ref[:, cols] = x[:, cols] + y * sc[:, cols]


def _pool_mixer(x, g, w, layer, scale, seq):
    n = x.shape[0]
    tm = TM_PROJ
    per_tile = tm // POOL_HALO
    last_halo_block = n // POOL_HALO - 1
    tok = lambda i: (i, 0)
    const = lambda i: (0, 0)
    return pl.pallas_call(
        functools.partial(_pool_kernel, seq=seq, tm=tm),
        grid=(n // tm,),
        in_specs=[pl.BlockSpec((POOL_HALO, D_MODEL), lambda i: (jnp.maximum(i * per_tile - 1, 0), 0)),
                  pl.BlockSpec((tm, D_MODEL), tok),
                  pl.BlockSpec((POOL_HALO, D_MODEL),
                               lambda i: (jnp.minimum((i + 1) * per_tile, last_halo_block), 0)),
                  pl.BlockSpec((1, D_MODEL), const),
                  _resident((None, len(POOL_WINDOWS), POOL_GROUP, POOL_GROUP),
                            lambda i: (layer, 0, 0, 0)),
                  pl.BlockSpec((1, D_MODEL), const)],
        out_specs=pl.BlockSpec((tm, D_MODEL), tok),
        out_shape=jax.ShapeDtypeStruct((n, D_MODEL), F32),
        compiler_params=_params("parallel"),
        name="pool_mixer",
    )(x, x, x, g, w, scale)


def _mlp_kernel(*refs, with_out_proj):
    if with_out_proj:
        x_ref, o_ref, wo_ref, g_ref, w1_ref, w2_ref, out_ref = refs
        x = x_ref[...] + jnp.dot(o_ref[...], wo_ref[...], preferred_element_type=F32)
    else:
        x_ref, g_ref, w1_ref, w2_ref, out_ref = refs
        x = x_ref[...]
    h = _rms(x, g_ref[...]).astype(BF16)
    acc = x
    for c in range(D_FF // D_MODEL):
        cols = slice(c * D_MODEL, (c + 1) * D_MODEL)
        hid = jnp.dot(h, w1_ref[:, cols], preferred_element_type=F32)
        act = jnp.square(jnp.maximum(hid, 0.0)).astype(BF16)
        acc = acc + jnp.dot(act, w2_ref[cols, :], preferred_element_type=F32)
    out_ref[...] = acc


def _mlp(x, g, w1, w2, layer, attn_out=None, w_o=None, attn_layer=None):
    n = x.shape[0]
    tm = TM_PROJ
    tok = lambda i: (i, 0)
    const = lambda i: (0, 0)
    with_out_proj = attn_out is not None
    operands = [x]
    in_specs = [pl.BlockSpec((tm, D_MODEL), tok)]
    if with_out_proj:
        operands += [attn_out, w_o]
        in_specs += [pl.BlockSpec((tm, D_MODEL), tok),
                     _resident((None, D_MODEL, D_MODEL), lambda i: (attn_layer, 0, 0))]
    operands += [g, w1, w2]
    in_specs += [pl.BlockSpec((1, D_MODEL), const),
                 _resident((None, D_MODEL, D_FF), lambda i: (layer, 0, 0)),
                 _resident((None, D_FF, D_MODEL), lambda i: (layer, 0, 0))]
    return pl.pallas_call(
        functools.partial(_mlp_kernel, with_out_proj=with_out_proj),
        grid=(n // tm,),
        in_specs=in_specs,
        out_specs=pl.BlockSpec((tm, D_MODEL), tok),
        out_shape=jax.ShapeDtypeStruct((n, D_MODEL), F32),
        compiler_params=_params("parallel"),
        name="out_proj_mlp" if with_out_proj else "sqrelu_mlp",
    )(*operands)


def kernel(x, positions, norm_mix, norm_mlp, attn_w_qkv, attn_w_o, attn_q_gain, attn_k_gain,
           attn_lam_q1, attn_lam_k1, attn_lam_q2, attn_lam_k2, attn_sub_gain,
           pool_w, pool_scale, mlp_w1, mlp_w2):
    batch, seq, d = x.shape
    assert d == D_MODEL and seq % TM_PROJ == 0 and seq % (TQ_ATTN * ATTN_TILES_PER_STEP) == 0
    n = batch * seq
    xf = x.reshape(n, d)

    cos, sin = _rope_tables(positions)
    scale = math.log2(math.e) / math.sqrt(QK_DIM)
    w_qkv_t = jnp.swapaxes(attn_w_qkv, 1, 2).astype(BF16)
    w_o = attn_w_o.astype(BF16)
    w_pool = pool_w.astype(BF16)
    w1 = mlp_w1.astype(BF16)
    w2 = mlp_w2.astype(BF16)

    for i in range(DEPTH):
        j = i // N_MIXERS
        g_mix = norm_mix[i].reshape(1, d)
        g_mlp = norm_mlp[i].reshape(1, d)
        if i % N_MIXERS == 0:
            q_gain = (attn_q_gain[j] * scale).reshape(QK_DIM, 1)
            k_gain = attn_k_gain[j].reshape(QK_DIM, 1)
            qt, k, vt = _qkv_proj(xf, g_mix, w_qkv_t, j, q_gain, k_gain, cos, sin)
            lam_vecs = jnp.stack([attn_lam_q1[j], attn_lam_k1[j], attn_lam_q2[j], attn_lam_k2[j]])
            attend = functools.partial(_attention, lambda_init=_lambda_init(i), batch=batch, seq=seq)
            o = lax.cond(_score_bound(q_gain, k_gain) <= FAST_SCORE_BOUND,
                         functools.partial(attend, subtract_max=False),
                         functools.partial(attend, subtract_max=True),
                         lam_vecs, qt, k, vt, attn_sub_gain[j].reshape(V_DIM, 1))
            xf = _mlp(xf, g_mlp, w1, w2, i, attn_out=o, w_o=w_o, attn_layer=j)
        else:
            xf = _pool_mixer(xf, g_mix, w_pool, j, pool_scale[j].reshape(1, d), seq)
            xf = _mlp(xf, g_mlp, w1, w2, i)
    return xf.reshape(batch, seq, d)
```

```python
import functools
import math

import jax
import jax.numpy as jnp
from jax import lax
from jax.experimental import pallas as pl
from jax.experimental.pallas import tpu as pltpu

D_MODEL = 1024
DEPTH = 4
N_MIXERS = 2
QK_DIM = 64
V_DIM = 2 * QK_DIM
N_HEADS = D_MODEL // V_DIM
ROPE_DIM = QK_DIM // 4
ROPE_THETA = 500000.0
POOL_WINDOWS = (2, 4, 8, 16)
POOL_GROUP = D_MODEL // len(POOL_WINDOWS)
D_FF = 4 * D_MODEL
EPS = 1e-6

LANES = 128
POOL_HALO = 8
FAST_SCORE_BOUND = 60.0
_NT_DIMS = (((1,), (1,)), ((), ()))
VMEM_LIMIT = 48 * 1024 * 1024

TM_PROJ = 512
MLP_ROW_BLOCK = 256
QKV_ROW_BLOCK = 512
TQ_ATTN = 256
ATTN_HEADS_PER_STEP = 2
TN_ROPE = 2048

F32 = jnp.float32
BF16 = jnp.bfloat16


def _lambda_init(layer_idx):
    return 0.8 - 0.6 * math.exp(-0.3 * layer_idx)


def _rms(x, g):
    ms = jnp.mean(x * x, axis=-1, keepdims=True)
    return x * lax.rsqrt(ms + EPS) * g


def _params(*sem):
    return pltpu.CompilerParams(dimension_semantics=sem, vmem_limit_bytes=VMEM_LIMIT)


def _resident(block_shape, index_map):
    return pl.BlockSpec(block_shape, index_map, pipeline_mode=pl.Buffered(1))


def _rope_kernel(pos_ref, invf_ref, cos_ref, sin_ref):
    ang = pos_ref[...].astype(F32) * invf_ref[...]
    cos_ref[...] = jnp.cos(ang)
    sin_ref[...] = jnp.sin(ang)


def _rope_tables(positions):
    n = positions.size
    half = ROPE_DIM // 2
    inv_freq = ROPE_THETA ** (-jnp.arange(0, ROPE_DIM, 2, dtype=F32) / ROPE_DIM)
    out = jax.ShapeDtypeStruct((half, n), F32)
    return pl.pallas_call(
        _rope_kernel,
        grid=(n // TN_ROPE,),
        in_specs=[pl.BlockSpec((1, TN_ROPE), lambda i: (0, i)),
                  pl.BlockSpec((half, 1), lambda i: (0, 0))],
        out_specs=[pl.BlockSpec((half, TN_ROPE), lambda i: (0, i))] * 2,
        out_shape=[out, out],
        compiler_params=_params("parallel"),
        name="rope_tables",
    )(positions.reshape(1, n), inv_freq.reshape(half, 1))


def _norm_rope_map(y, gain, cos, sin):
    half = ROPE_DIM // 2
    inv = lax.rsqrt(jnp.mean(y * y, axis=0, keepdims=True) + EPS)
    n = y * inv * gain
    n1 = n[0:half]
    n2 = n[half:ROPE_DIM]
    return jnp.concatenate([n1 * cos - n2 * sin, n2 * cos + n1 * sin, n[ROPE_DIM:]], axis=0)


def _qkv_kernel(x_ref, g_ref, wt_ref, qg_ref, kg_ref, cos_ref, sin_ref, qt_ref, k_ref, vt_ref):
    h = _rms(x_ref[...], g_ref[...]).astype(BF16)
    cos = cos_ref[...]
    sin = sin_ref[...]
    qg = qg_ref[...]
    kg = kg_ref[...]

    def project(r0):
        return lax.dot_general(wt_ref[r0:r0 + QKV_ROW_BLOCK, :], h, _NT_DIMS,
                               preferred_element_type=F32)

    def heads(yt):
        for hd in range(QKV_ROW_BLOCK // V_DIM):
            yield hd * V_DIM, [yt[hd * V_DIM + m * QK_DIM:hd * V_DIM + (m + 1) * QK_DIM]
                               for m in range(V_DIM // QK_DIM)]

    for r0 in range(0, D_MODEL, QKV_ROW_BLOCK):
        for off, maps in heads(project(r0)):
            for m, y in enumerate(maps):
                rows = slice(r0 + off + m * QK_DIM, r0 + off + (m + 1) * QK_DIM)
                qt_ref[rows, :] = _norm_rope_map(y, qg, cos, sin).astype(BF16)
    for r0 in range(0, D_MODEL, QKV_ROW_BLOCK):
        for off, maps in heads(project(D_MODEL + r0)):
            kt = jnp.concatenate([_norm_rope_map(y, kg, cos, sin) for y in maps], axis=0)
            k_ref[:, r0 + off:r0 + off + V_DIM] = kt.T.astype(BF16)
    for r0 in range(0, D_MODEL, QKV_ROW_BLOCK):
        vt_ref[r0:r0 + QKV_ROW_BLOCK, :] = project(2 * D_MODEL + r0).astype(BF16)


def _qkv_proj(x, g, w_t, layer, q_gain, k_gain, cos, sin):
    n = x.shape[0]
    tm = TM_PROJ
    half = ROPE_DIM // 2
    tok = lambda i: (i, 0)
    tok_t = lambda i: (0, i)
    const = lambda i: (0, 0)
    feat_major = jax.ShapeDtypeStruct((D_MODEL, n), BF16)
    return pl.pallas_call(
        _qkv_kernel,
        grid=(n // tm,),
        in_specs=[pl.BlockSpec((tm, D_MODEL), tok),
                  pl.BlockSpec((1, D_MODEL), const),
                  _resident((None, 3 * D_MODEL, D_MODEL), lambda i: (layer, 0, 0)),
                  pl.BlockSpec((QK_DIM, 1), const),
                  pl.BlockSpec((QK_DIM, 1), const),
                  pl.BlockSpec((half, tm), tok_t),
                  pl.BlockSpec((half, tm), tok_t)],
        out_specs=[pl.BlockSpec((D_MODEL, tm), tok_t),
                   pl.BlockSpec((tm, D_MODEL), tok),
                   pl.BlockSpec((D_MODEL, tm), tok_t)],
        out_shape=[feat_major, jax.ShapeDtypeStruct((n, D_MODEL), BF16), feat_major],
        compiler_params=_params("parallel"),
        name="qkv_proj",
    )(x, g, w_t, q_gain, k_gain, cos, sin)


def _attn_kernel(lam_ref, qt_ref, k_ref, vt_ref, sg_ref, o_ref, *, lambda_init, tq, subtract_max):
    lv = lam_ref[...]
    lam = (jnp.exp(jnp.sum(lv[0:1] * lv[1:2], axis=-1, keepdims=True))
           - jnp.exp(jnp.sum(lv[2:3] * lv[3:4], axis=-1, keepdims=True)) + lambda_init)
    sg = sg_ref[...]

    def probabilities(hd, t):
        head = slice(hd * V_DIM, (hd + 1) * V_DIM)
        qt = qt_ref[head, t * tq:(t + 1) * tq]
        zero = jnp.zeros((QK_DIM, tq), BF16)
        q_both = jnp.concatenate([jnp.concatenate([qt[0:QK_DIM], zero], axis=1),
                                  jnp.concatenate([zero, qt[QK_DIM:]], axis=1)], axis=0)
        st = jnp.dot(k_ref[:, head], q_both, preferred_element_type=F32)
        if subtract_max:
            st = st - jnp.max(st, axis=0, keepdims=True)
        p = jnp.exp2(st)
        return p.astype(BF16), jnp.sum(p, axis=0, keepdims=True)

    def finish(hd, t, p, l):
        head = slice(hd * V_DIM, (hd + 1) * V_DIM)
        acc = jnp.dot(vt_ref[head, :], p, preferred_element_type=F32)
        c1 = 1.0 / l[:, 0:tq]
        c2 = lam / l[:, tq:2 * tq]
        ot = acc[:, 0:tq] * c1 - acc[:, tq:2 * tq] * c2
        ms = jnp.mean(ot * ot, axis=0, keepdims=True)
        ot = ot * lax.rsqrt(ms + EPS) * sg * (1.0 - lambda_init)
        o_ref[t * tq:(t + 1) * tq, head] = ot.T.astype(BF16)

    work = [(hd, t) for hd in range(qt_ref.shape[0] // V_DIM) for t in range(qt_ref.shape[1] // tq)]
    nxt = probabilities(*work[0])
    for i, item in enumerate(work):
        cur = nxt
        if i + 1 < len(work):
            nxt = probabilities(*work[i + 1])
        finish(*item, *cur)


def _score_bound(q_gain, k_gain):
    return QK_DIM * jnp.max(jnp.abs(q_gain)) * jnp.max(jnp.abs(k_gain))


def _attention(lam_vecs, qt, k, vt, sub_gain, lambda_init, batch, seq, subtract_max):
    tq = TQ_ATTN
    assert seq % tq == 0 and N_HEADS % ATTN_HEADS_PER_STEP == 0
    width = V_DIM * ATTN_HEADS_PER_STEP
    const = lambda b, h: (0, 0)
    return pl.pallas_call(
        functools.partial(_attn_kernel, lambda_init=lambda_init, tq=tq, subtract_max=subtract_max),
        grid=(batch, N_HEADS // ATTN_HEADS_PER_STEP),
        in_specs=[pl.BlockSpec((4, QK_DIM), const),
                  pl.BlockSpec((width, seq), lambda b, h: (h, b)),
                  pl.BlockSpec((seq, width), lambda b, h: (b, h)),
                  pl.BlockSpec((width, seq), lambda b, h: (h, b)),
                  pl.BlockSpec((V_DIM, 1), const)],
        out_specs=pl.BlockSpec((seq, width), lambda b, h: (b, h)),
        out_shape=jax.ShapeDtypeStruct((batch * seq, D_MODEL), BF16),
        compiler_params=_params("parallel", "parallel"),
        name="diff_attention" if subtract_max else "diff_attention_bounded",
    )(lam_vecs, qt, k, vt, sub_gain)


def _pooled_column_groups(xp_ref, x_ref, xn_ref, g, w_ref, sc, r0, n_blk, seq):
    tm = x_ref.shape[0]
    tiles_per_seq = seq // tm
    j = pl.program_id(0) % tiles_per_seq
    x = x_ref[r0:r0 + n_blk, :]
    h = _rms(x, g)
    if r0 == 0:
        above = jnp.where(j > 0, _rms(xp_ref[...], g), 0.0)
    else:
        above = _rms(x_ref[r0 - POOL_HALO:r0, :], g)
    if r0 + n_blk == tm:
        below = jnp.where(j < tiles_per_seq - 1, _rms(xn_ref[...], g), 0.0)
    else:
        below = _rms(x_ref[r0 + n_blk:r0 + n_blk + POOL_HALO, :], g)
    hh = jnp.concatenate([above, h, below], axis=0)
    n_rows = n_blk + 2 * POOL_HALO
    t = j * tm + r0 + lax.broadcasted_iota(jnp.int32, (n_blk, 1), 0)
    for gi, win in enumerate(POOL_WINDOWS):
        r = win // 2
        cols = slice(gi * POOL_GROUP, (gi + 1) * POOL_GROUP)
        run = hh[:, cols]
        span = 1
        while span < win:
            run = run + pltpu.roll(run, span, 0)
            span *= 2
        if r > 1:
            run = pltpu.roll(run, n_rows - (r - 1), 0)
        tot = run[POOL_HALO:POOL_HALO + n_blk, :]
        cnt = (jnp.minimum(t + r, seq) - jnp.maximum(t - r, 0)).astype(F32)
        diff = (tot / cnt - h[:, cols]).astype(BF16)
        y = jnp.dot(diff, w_ref[gi], preferred_element_type=F32)
        yield x[:, cols] + y * sc[:, cols]


def _mixer_mlp_kernel(*refs, mixer, seq):
    if mixer == "attn":
        x_ref, o_ref, wo_ref, g_ref, w1_ref, w2_ref, out_ref = refs

        def mixed_columns(r0):
            rows = slice(r0, r0 + MLP_ROW_BLOCK)
            yield x_ref[rows, :] + jnp.dot(o_ref[rows, :], wo_ref[...], preferred_element_type=F32)
    else:
        xp_ref, x_ref, xn_ref, gmix_ref, wpool_ref, sc_ref, g_ref, w1_ref, w2_ref, out_ref = refs
        gmix = gmix_ref[...]
        sc = sc_ref[...]

        def mixed_columns(r0):
            return _pooled_column_groups(xp_ref, x_ref, xn_ref, gmix, wpool_ref, sc, r0,
                                         MLP_ROW_BLOCK, seq)
    g = g_ref[...]
    blocks = list(range(0, x_ref.shape[0], MLP_ROW_BLOCK))
    cur = list(mixed_columns(blocks[0]))
    for b, r0 in enumerate(blocks):
        x = jnp.concatenate(cur, axis=1) if len(cur) > 1 else cur[0]
        upcoming = mixed_columns(blocks[b + 1]) if b + 1 < len(blocks) else iter(())
        cur = []
        h = _rms(x, g).astype(BF16)
        acc = x
        for c in range(D_FF // D_MODEL):
            cols = slice(c * D_MODEL, (c + 1) * D_MODEL)
            hid = jnp.dot(h, w1_ref[:, cols], preferred_element_type=F32)
            act = jnp.square(jnp.maximum(hid, 0.0)).astype(BF16)
            acc = acc + jnp.dot(act, w2_ref[cols, :], preferred_element_type=F32)
            cur.extend(piece for piece in [next(upcoming, None)] if piece is not None)
        cur.extend(upcoming)
        out_ref[r0:r0 + MLP_ROW_BLOCK, :] = acc


def _mixer_mlp(x, g, w1, w2, layer, seq, *, attn_out=None, w_o=None, g_mix=None, w_pool=None,
               pool_scale=None, mixer_layer=None):
    n = x.shape[0]
    tm = TM_PROJ
    tok = lambda i: (i, 0)
    const = lambda i: (0, 0)
    if attn_out is not None:
        mixer = "attn"
        operands = [x, attn_out, w_o]
        in_specs = [pl.BlockSpec((tm, D_MODEL), tok),
                    pl.BlockSpec((tm, D_MODEL), tok),
                    _resident((None, D_MODEL, D_MODEL), lambda i: (mixer_layer, 0, 0))]
    else:
        mixer = "pool"
        per_tile = tm // POOL_HALO
        last_halo_block = n // POOL_HALO - 1
        operands = [x, x, x, g_mix, w_pool, pool_scale]
        in_specs = [pl.BlockSpec((POOL_HALO, D_MODEL), lambda i: (jnp.maximum(i * per_tile - 1, 0), 0)),
                    pl.BlockSpec((tm, D_MODEL), tok),
                    pl.BlockSpec((POOL_HALO, D_MODEL),
                                 lambda i: (jnp.minimum((i + 1) * per_tile, last_halo_block), 0)),
                    pl.BlockSpec((1, D_MODEL), const),
                    _resident((None, len(POOL_WINDOWS), POOL_GROUP, POOL_GROUP),
                              lambda i: (mixer_layer, 0, 0, 0)),
                    pl.BlockSpec((1, D_MODEL), const)]
    operands += [g, w1, w2]
    in_specs += [pl.BlockSpec((1, D_MODEL), const),
                 _resident((None, D_MODEL, D_FF), lambda i: (layer, 0, 0)),
                 _resident((None, D_FF, D_MODEL), lambda i: (layer, 0, 0))]
    return pl.pallas_call(
        functools.partial(_mixer_mlp_kernel, mixer=mixer, seq=seq),
        grid=(n // tm,),
        in_specs=in_specs,
        out_specs=pl.BlockSpec((tm, D_MODEL), tok),
        out_shape=jax.ShapeDtypeStruct((n, D_MODEL), F32),
        compiler_params=_params("parallel"),
        name="out_proj_mlp" if mixer == "attn" else "pool_mixer_mlp",
    )(*operands)


def kernel(x, positions, norm_mix, norm_mlp, attn_w_qkv, attn_w_o, attn_q_gain, attn_k_gain,
           attn_lam_q1, attn_lam_k1, attn_lam_q2, attn_lam_k2, attn_sub_gain,
           pool_w, pool_scale, mlp_w1, mlp_w2):
    batch, seq, d = x.shape
    assert d == D_MODEL and seq % TM_PROJ == 0
    n = batch * seq
    xf = x.reshape(n, d)

    cos, sin = _rope_tables(positions)
    scale = math.log2(math.e) / math.sqrt(QK_DIM)
    w_qkv_t = jnp.swapaxes(attn_w_qkv, 1, 2).astype(BF16)
    w_o = attn_w_o.astype(BF16)
    w_pool = pool_w.astype(BF16)
    w1 = mlp_w1.astype(BF16)
    w2 = mlp_w2.astype(BF16)

    for i in range(DEPTH):
        j = i // N_MIXERS
        g_mix = norm_mix[i].reshape(1, d)
        g_mlp = norm_mlp[i].reshape(1, d)
        if i % N_MIXERS == 0:
            q_gain = (attn_q_gain[j] * scale).reshape(QK_DIM, 1)
            k_gain = attn_k_gain[j].reshape(QK_DIM, 1)
            qt, k, vt = _qkv_proj(xf, g_mix, w_qkv_t, j, q_gain, k_gain, cos, sin)
            lam_vecs = jnp.stack([attn_lam_q1[j], attn_lam_k1[j], attn_lam_q2[j], attn_lam_k2[j]])
            attend = functools.partial(_attention, lambda_init=_lambda_init(i), batch=batch, seq=seq)
            o = lax.cond(_score_bound(q_gain, k_gain) <= FAST_SCORE_BOUND,
                         functools.partial(attend, subtract_max=False),
                         functools.partial(attend, subtract_max=True),
                         lam_vecs, qt, k, vt, attn_sub_gain[j].reshape(V_DIM, 1))
            xf = _mixer_mlp(xf, g_mlp, w1, w2, i, seq, attn_out=o, w_o=w_o, mixer_layer=j)
        else:
            xf = _mixer_mlp(xf, g_mlp, w1, w2, i, seq, g_mix=g_mix, w_pool=w_pool,
                            pool_scale=pool_scale[j].reshape(1, d), mixer_layer=j)
    return xf.reshape(batch, seq, d)
```

```python
import functools
import math

import jax
import jax.numpy as jnp
from jax import lax
from jax.experimental import pallas as pl
from jax.experimental.pallas import tpu as pltpu

D_MODEL = 1024
DEPTH = 4
N_MIXERS = 2
QK_DIM = 64
V_DIM = 2 * QK_DIM
N_HEADS = D_MODEL // V_DIM
ROPE_DIM = QK_DIM // 4
ROPE_THETA = 500000.0
POOL_WINDOWS = (2, 4, 8, 16)
POOL_GROUP = D_MODEL // len(POOL_WINDOWS)
D_FF = 4 * D_MODEL
EPS = 1e-6

LANES = 128
POOL_HALO = 8
FAST_SCORE_BOUND = 60.0
_NT_DIMS = (((1,), (1,)), ((), ()))
VMEM_LIMIT = 48 * 1024 * 1024

TM_PROJ = 512
MLP_ROW_BLOCK = 256
QKV_ROW_BLOCK = 512
TQ_ATTN = 256
ATTN_HEADS_PER_STEP = 2
TN_ROPE = 2048

F32 = jnp.float32
BF16 = jnp.bfloat16


def _lambda_init(layer_idx):
    return 0.8 - 0.6 * math.exp(-0.3 * layer_idx)


def _rms(x, g):
    ms = jnp.mean(x * x, axis=-1, keepdims=True)
    return x * lax.rsqrt(ms + EPS) * g


def _params(*sem):
    return pltpu.CompilerParams(dimension_semantics=sem, vmem_limit_bytes=VMEM_LIMIT)


def _resident(block_shape, index_map):
    return pl.BlockSpec(block_shape, index_map, pipeline_mode=pl.Buffered(1))


def _rope_kernel(pos_ref, invf_ref, cos_ref, sin_ref):
    ang = pos_ref[...].astype(F32) * invf_ref[...]
    cos_ref[...] = jnp.cos(ang)
    sin_ref[...] = jnp.sin(ang)


def _rope_tables(positions):
    n = positions.size
    half = ROPE_DIM // 2
    inv_freq = ROPE_THETA ** (-jnp.arange(0, ROPE_DIM, 2, dtype=F32) / ROPE_DIM)
    out = jax.ShapeDtypeStruct((half, n), F32)
    return pl.pallas_call(
        _rope_kernel,
        grid=(n // TN_ROPE,),
        in_specs=[pl.BlockSpec((1, TN_ROPE), lambda i: (0, i)),
                  pl.BlockSpec((half, 1), lambda i: (0, 0))],
        out_specs=[pl.BlockSpec((half, TN_ROPE), lambda i: (0, i))] * 2,
        out_shape=[out, out],
        compiler_params=_params("parallel"),
        name="rope_tables",
    )(positions.reshape(1, n), inv_freq.reshape(half, 1))


def _norm_rope_map(y, gain, cos, sin):
    half = ROPE_DIM // 2
    inv = lax.rsqrt(jnp.mean(y * y, axis=0, keepdims=True) + EPS)
    n = y * inv * gain
    n1 = n[0:half]
    n2 = n[half:ROPE_DIM]
    return jnp.concatenate([n1 * cos - n2 * sin, n2 * cos + n1 * sin, n[ROPE_DIM:]], axis=0)


def _qkv_kernel(x_ref, g_ref, wt_ref, qg_ref, kg_ref, cos_ref, sin_ref, qt_ref, k_ref, vt_ref):
    h = _rms(x_ref[...], g_ref[...]).astype(BF16)
    cos = cos_ref[...]
    sin = sin_ref[...]
    qg = qg_ref[...]
    kg = kg_ref[...]

    def project(r0):
        return lax.dot_general(wt_ref[r0:r0 + QKV_ROW_BLOCK, :], h, _NT_DIMS,
                               preferred_element_type=F32)

    def heads(yt):
        for hd in range(QKV_ROW_BLOCK // V_DIM):
            yield hd * V_DIM, [yt[hd * V_DIM + m * QK_DIM:hd * V_DIM + (m + 1) * QK_DIM]
                               for m in range(V_DIM // QK_DIM)]

    def finish(r0, yt):
        if r0 < D_MODEL:
            for off, maps in heads(yt):
                for m, y in enumerate(maps):
                    rows = slice(r0 + off + m * QK_DIM, r0 + off + (m + 1) * QK_DIM)
                    qt_ref[rows, :] = _norm_rope_map(y, qg, cos, sin).astype(BF16)
        elif r0 < 2 * D_MODEL:
            for off, maps in heads(yt):
                kt = jnp.concatenate([_norm_rope_map(y, kg, cos, sin) for y in maps], axis=0)
                c0 = r0 - D_MODEL + off
                k_ref[:, c0:c0 + V_DIM] = kt.T.astype(BF16)
        else:
            vt_ref[r0 - 2 * D_MODEL:r0 - 2 * D_MODEL + QKV_ROW_BLOCK, :] = yt.astype(BF16)

    starts = list(range(0, 3 * D_MODEL, QKV_ROW_BLOCK))
    nxt = project(starts[0])
    for i, r0 in enumerate(starts):
        cur = nxt
        if i + 1 < len(starts):
            nxt = project(starts[i + 1])
        finish(r0, cur)


def _qkv_proj(x, g, w_t, layer, q_gain, k_gain, cos, sin):
    n = x.shape[0]
    tm = TM_PROJ
    half = ROPE_DIM // 2
    tok = lambda i: (i, 0)
    tok_t = lambda i: (0, i)
    const = lambda i: (0, 0)
    feat_major = jax.ShapeDtypeStruct((D_MODEL, n), BF16)
    return pl.pallas_call(
        _qkv_kernel,
        grid=(n // tm,),
        in_specs=[pl.BlockSpec((tm, D_MODEL), tok),
                  pl.BlockSpec((1, D_MODEL), const),
                  _resident((None, 3 * D_MODEL, D_MODEL), lambda i: (layer, 0, 0)),
                  pl.BlockSpec((QK_DIM, 1), const),
                  pl.BlockSpec((QK_DIM, 1), const),
                  pl.BlockSpec((half, tm), tok_t),
                  pl.BlockSpec((half, tm), tok_t)],
        out_specs=[pl.BlockSpec((D_MODEL, tm), tok_t),
                   pl.BlockSpec((tm, D_MODEL), tok),
                   pl.BlockSpec((D_MODEL, tm), tok_t)],
        out_shape=[feat_major, jax.ShapeDtypeStruct((n, D_MODEL), BF16), feat_major],
        compiler_params=_params("parallel"),
        name="qkv_proj",
    )(x, g, w_t, q_gain, k_gain, cos, sin)


def _attn_kernel(lam_ref, qt_ref, k_ref, vt_ref, sg_ref, o_ref, *, lambda_init, tq, subtract_max):
    lv = lam_ref[...]
    lam = (jnp.exp(jnp.sum(lv[0:1] * lv[1:2], axis=-1, keepdims=True))
           - jnp.exp(jnp.sum(lv[2:3] * lv[3:4], axis=-1, keepdims=True)) + lambda_init)
    sg = sg_ref[...]

    def probabilities(hd, t):
        head = slice(hd * V_DIM, (hd + 1) * V_DIM)
        qt = qt_ref[head, t * tq:(t + 1) * tq]
        zero = jnp.zeros((QK_DIM, tq), BF16)
        q_both = jnp.concatenate([jnp.concatenate([qt[0:QK_DIM], zero], axis=1),
                                  jnp.concatenate([zero, qt[QK_DIM:]], axis=1)], axis=0)
        st = jnp.dot(k_ref[:, head], q_both, preferred_element_type=F32)
        if subtract_max:
            st = st - jnp.max(st, axis=0, keepdims=True)
        p = jnp.exp2(st)
        return p.astype(BF16), jnp.sum(p, axis=0, keepdims=True)

    def finish(hd, t, p, l):
        head = slice(hd * V_DIM, (hd + 1) * V_DIM)
        acc = jnp.dot(vt_ref[head, :], p, preferred_element_type=F32)
        c1 = 1.0 / l[:, 0:tq]
        c2 = lam / l[:, tq:2 * tq]
        ot = acc[:, 0:tq] * c1 - acc[:, tq:2 * tq] * c2
        ms = jnp.mean(ot * ot, axis=0, keepdims=True)
        ot = ot * lax.rsqrt(ms + EPS) * sg * (1.0 - lambda_init)
        o_ref[t * tq:(t + 1) * tq, head] = ot.T.astype(BF16)

    work = [(hd, t) for hd in range(qt_ref.shape[0] // V_DIM) for t in range(qt_ref.shape[1] // tq)]
    nxt = probabilities(*work[0])
    for i, item in enumerate(work):
        cur = nxt
        if i + 1 < len(work):
            nxt = probabilities(*work[i + 1])
        finish(*item, *cur)


def _score_bound(q_gain, k_gain):
    return QK_DIM * jnp.max(jnp.abs(q_gain)) * jnp.max(jnp.abs(k_gain))


def _attention(lam_vecs, qt, k, vt, sub_gain, lambda_init, batch, seq, subtract_max):
    tq = TQ_ATTN
    assert seq % tq == 0 and N_HEADS % ATTN_HEADS_PER_STEP == 0
    width = V_DIM * ATTN_HEADS_PER_STEP
    const = lambda b, h: (0, 0)
    return pl.pallas_call(
        functools.partial(_attn_kernel, lambda_init=lambda_init, tq=tq, subtract_max=subtract_max),
        grid=(batch, N_HEADS // ATTN_HEADS_PER_STEP),
        in_specs=[pl.BlockSpec((4, QK_DIM), const),
                  pl.BlockSpec((width, seq), lambda b, h: (h, b)),
                  pl.BlockSpec((seq, width), lambda b, h: (b, h)),
                  pl.BlockSpec((width, seq), lambda b, h: (h, b)),
                  pl.BlockSpec((V_DIM, 1), const)],
        out_specs=pl.BlockSpec((seq, width), lambda b, h: (b, h)),
        out_shape=jax.ShapeDtypeStruct((batch * seq, D_MODEL), BF16),
        compiler_params=_params("parallel", "parallel"),
        name="diff_attention" if subtract_max else "diff_attention_bounded",
    )(lam_vecs, qt, k, vt, sub_gain)


def _pooled_column_groups(rows, above, below, g, w_ref, sc, t0, seq):
    x = rows()
    n_blk = x.shape[0]
    h = _rms(x, g)
    halo = []
    for neighbour in (above, below):
        nb_rows, in_sequence = neighbour()
        nb_h = _rms(nb_rows, g)
        halo.append(nb_h if in_sequence is True else jnp.where(in_sequence, nb_h, 0.0))
    hh = jnp.concatenate([halo[0], h, halo[1]], axis=0)
    n_rows = n_blk + 2 * POOL_HALO
    t = t0 + lax.broadcasted_iota(jnp.int32, (n_blk, 1), 0)
    for gi, win in enumerate(POOL_WINDOWS):
        r = win // 2
        cols = slice(gi * POOL_GROUP, (gi + 1) * POOL_GROUP)
        run = hh[:, cols]
        span = 1
        while span < win:
            run = run + pltpu.roll(run, span, 0)
            span *= 2
        if r > 1:
            run = pltpu.roll(run, n_rows - (r - 1), 0)
        tot = run[POOL_HALO:POOL_HALO + n_blk, :]
        cnt = (jnp.minimum(t + r, seq) - jnp.maximum(t - r, 0)).astype(F32)
        diff = (tot / cnt - h[:, cols]).astype(BF16)
        y = jnp.dot(diff, w_ref[gi], preferred_element_type=F32)
        yield x[:, cols] + y * sc[:, cols]


def _mixer_mlp_kernel(*refs, mixer, seq):
    rb = MLP_ROW_BLOCK
    if mixer == "attn":
        x_ref, o_ref, wo_ref, g_ref, w1_ref, w2_ref, out_ref = refs
        tm = x_ref.shape[0]

        def mixed_columns(r0):
            if r0 >= tm:
                return
            rows = slice(r0, r0 + rb)
            yield x_ref[rows, :] + jnp.dot(o_ref[rows, :], wo_ref[...], preferred_element_type=F32)

        first = list(mixed_columns(0))
    else:
        (xp_ref, x_ref, xn_ref, xnext_ref, gmix_ref, wpool_ref, sc_ref, g_ref, w1_ref, w2_ref,
         out_ref, carry_ref) = refs
        tm = x_ref.shape[0]
        gmix = gmix_ref[...]
        sc = sc_ref[...]
        tiles_per_seq = seq // tm
        step = pl.program_id(0)
        j = step % tiles_per_seq
        j_next = (step + 1) % tiles_per_seq

        def mixed_columns(r0):
            if r0 == tm:
                return _pooled_column_groups(
                    lambda: xnext_ref[0:rb, :],
                    lambda: (x_ref[tm - POOL_HALO:tm, :], j_next > 0),
                    lambda: (xnext_ref[rb:rb + POOL_HALO, :], True),
                    gmix, wpool_ref, sc, j_next * tm, seq)
            above = ((lambda: (xp_ref[...], j > 0)) if r0 == 0
                     else (lambda: (x_ref[r0 - POOL_HALO:r0, :], True)))
            below = ((lambda: (xn_ref[...], j < tiles_per_seq - 1)) if r0 + rb == tm
                     else (lambda: (x_ref[r0 + rb:r0 + rb + POOL_HALO, :], True)))
            return _pooled_column_groups(lambda: x_ref[r0:r0 + rb, :], above, below,
                                         gmix, wpool_ref, sc, j * tm + r0, seq)

        @pl.when(step == 0)
        def _():
            carry_ref[...] = jnp.concatenate(list(mixed_columns(0)), axis=1)

        first = [carry_ref[...]]
    g = g_ref[...]
    cur = first
    for r0 in range(0, tm, rb):
        x = jnp.concatenate(cur, axis=1) if len(cur) > 1 else cur[0]
        upcoming = mixed_columns(r0 + rb)
        cur = []
        h = _rms(x, g).astype(BF16)
        acc = x
        for c in range(D_FF // D_MODEL):
            cols = slice(c * D_MODEL, (c + 1) * D_MODEL)
            hid = jnp.dot(h, w1_ref[:, cols], preferred_element_type=F32)
            act = jnp.square(jnp.maximum(hid, 0.0)).astype(BF16)
            acc = acc + jnp.dot(act, w2_ref[cols, :], preferred_element_type=F32)
            cur.extend(piece for piece in [next(upcoming, None)] if piece is not None)
        cur.extend(upcoming)
        out_ref[r0:r0 + rb, :] = acc
    if mixer == "pool":
        carry_ref[...] = jnp.concatenate(cur, axis=1)


def _mixer_mlp(x, g, w1, w2, layer, seq, *, attn_out=None, w_o=None, g_mix=None, w_pool=None,
               pool_scale=None, mixer_layer=None):
    n = x.shape[0]
    tm = TM_PROJ
    tok = lambda i: (i, 0)
    const = lambda i: (0, 0)
    if attn_out is not None:
        mixer = "attn"
        operands = [x, attn_out, w_o]
        in_specs = [pl.BlockSpec((tm, D_MODEL), tok),
                    pl.BlockSpec((tm, D_MODEL), tok),
                    _resident((None, D_MODEL, D_MODEL), lambda i: (mixer_layer, 0, 0))]
    else:
        mixer = "pool"
        per_tile = tm // POOL_HALO
        last_halo_block = n // POOL_HALO - 1
        last_tile = n // tm - 1
        operands = [x, x, x, x, g_mix, w_pool, pool_scale]
        in_specs = [pl.BlockSpec((POOL_HALO, D_MODEL), lambda i: (jnp.maximum(i * per_tile - 1, 0), 0)),
                    pl.BlockSpec((tm, D_MODEL), tok),
                    pl.BlockSpec((POOL_HALO, D_MODEL),
                                 lambda i: (jnp.minimum((i + 1) * per_tile, last_halo_block), 0)),
                    pl.BlockSpec((tm, D_MODEL), lambda i: (jnp.minimum(i + 1, last_tile), 0)),
                    pl.BlockSpec((1, D_MODEL), const),
                    _resident((None, len(POOL_WINDOWS), POOL_GROUP, POOL_GROUP),
                              lambda i: (mixer_layer, 0, 0, 0)),
                    pl.BlockSpec((1, D_MODEL), const)]
    operands += [g, w1, w2]
    in_specs += [pl.BlockSpec((1, D_MODEL), const),
                 _resident((None, D_MODEL, D_FF), lambda i: (layer, 0, 0)),
                 _resident((None, D_FF, D_MODEL), lambda i: (layer, 0, 0))]
    return pl.pallas_call(
        functools.partial(_mixer_mlp_kernel, mixer=mixer, seq=seq),
        grid=(n // tm,),
        in_specs=in_specs,
        out_specs=pl.BlockSpec((tm, D_MODEL), tok),
        out_shape=jax.ShapeDtypeStruct((n, D_MODEL), F32),
        scratch_shapes=[] if mixer == "attn" else [pltpu.VMEM((MLP_ROW_BLOCK, D_MODEL), F32)],
        compiler_params=_params("parallel" if mixer == "attn" else "arbitrary"),
        name="out_proj_mlp" if mixer == "attn" else "pool_mixer_mlp",
    )(*operands)


def kernel(x, positions, norm_mix, norm_mlp, attn_w_qkv, attn_w_o, attn_q_gain, attn_k_gain,
           attn_lam_q1, attn_lam_k1, attn_lam_q2, attn_lam_k2, attn_sub_gain,
           pool_w, pool_scale, mlp_w1, mlp_w2):
    batch, seq, d = x.shape
    assert d == D_MODEL and seq % TM_PROJ == 0
    n = batch * seq
    xf = x.reshape(n, d)

    cos, sin = _rope_tables(positions)
    scale = math.log2(math.e) / math.sqrt(QK_DIM)
    w_qkv_t = jnp.swapaxes(attn_w_qkv, 1, 2).astype(BF16)
    w_o = attn_w_o.astype(BF16)
    w_pool = pool_w.astype(BF16)
    w1 = mlp_w1.astype(BF16)
    w2 = mlp_w2.astype(BF16)

    for i in range(DEPTH):
        j = i // N_MIXERS
        g_mix = norm_mix[i].reshape(1, d)
        g_mlp = norm_mlp[i].reshape(1, d)
        if i % N_MIXERS == 0:
            q_gain = (attn_q_gain[j] * scale).reshape(QK_DIM, 1)
            k_gain = attn_k_gain[j].reshape(QK_DIM, 1)
            qt, k, vt = _qkv_proj(xf, g_mix, w_qkv_t, j, q_gain, k_gain, cos, sin)
            lam_vecs = jnp.stack([attn_lam_q1[j], attn_lam_k1[j], attn_lam_q2[j], attn_lam_k2[j]])
            attend = functools.partial(_attention, lambda_init=_lambda_init(i), batch=batch, seq=seq)
            o = lax.cond(_score_bound(q_gain, k_gain) <= FAST_SCORE_BOUND,
                         functools.partial(attend, subtract_max=False),
                         functools.partial(attend, subtract_max=True),
                         lam_vecs, qt, k, vt, attn_sub_gain[j].reshape(V_DIM, 1))
            xf = _mixer_mlp(xf, g_mlp, w1, w2, i, seq, attn_out=o, w_o=w_o, mixer_layer=j)
        else:
            xf = _mixer_mlp(xf, g_mlp, w1, w2, i, seq, g_mix=g_mix, w_pool=w_pool,
                            pool_scale=pool_scale[j].reshape(1, d), mixer_layer=j)
    return xf.reshape(batch, seq, d)
```

```python
import functools
import math

import jax
import jax.numpy as jnp
from jax import lax
from jax.experimental import pallas as pl
from jax.experimental.pallas import tpu as pltpu

D_MODEL = 1024
DEPTH = 4
N_MIXERS = 2
QK_DIM = 64
V_DIM = 2 * QK_DIM
N_HEADS = D_MODEL // V_DIM
ROPE_DIM = QK_DIM // 4
ROPE_THETA = 500000.0
POOL_WINDOWS = (2, 4, 8, 16)
POOL_GROUP = D_MODEL // len(POOL_WINDOWS)
D_FF = 4 * D_MODEL
EPS = 1e-6

LANES = 128
POOL_HALO = 8
FAST_SCORE_BOUND = 60.0
_NT_DIMS = (((1,), (1,)), ((), ()))
_NN_DIMS = (((1,), (0,)), ((), ()))
VMEM_LIMIT = 56 * 1024 * 1024

TM_PROJ = 512
MLP_ROW_BLOCK = 256
QKV_ROW_BLOCK = 512
TQ_ATTN = 256
ATTN_HEADS_PER_STEP = 2
TN_ROPE = 2048

F32 = jnp.float32
BF16 = jnp.bfloat16


def _lambda_init(layer_idx):
    return 0.8 - 0.6 * math.exp(-0.3 * layer_idx)


def _rms(x, g):
    ms = jnp.mean(x * x, axis=-1, keepdims=True)
    return x * lax.rsqrt(ms + EPS) * g


def _params(*sem):
    return pltpu.CompilerParams(dimension_semantics=sem, vmem_limit_bytes=VMEM_LIMIT)


def _resident(block_shape, index_map):
    return pl.BlockSpec(block_shape, index_map, pipeline_mode=pl.Buffered(1))


def _rope_kernel(pos_ref, invf_ref, cos_ref, sin_ref):
    ang = pos_ref[...].astype(F32) * invf_ref[...]
    cos_ref[...] = jnp.cos(ang)
    sin_ref[...] = jnp.sin(ang)


def _rope_tables(positions):
    n = positions.size
    half = ROPE_DIM // 2
    inv_freq = ROPE_THETA ** (-jnp.arange(0, ROPE_DIM, 2, dtype=F32) / ROPE_DIM)
    out = jax.ShapeDtypeStruct((half, n), F32)
    return pl.pallas_call(
        _rope_kernel,
        grid=(n // TN_ROPE,),
        in_specs=[pl.BlockSpec((1, TN_ROPE), lambda i: (0, i)),
                  pl.BlockSpec((half, 1), lambda i: (0, 0))],
        out_specs=[pl.BlockSpec((half, TN_ROPE), lambda i: (0, i))] * 2,
        out_shape=[out, out],
        compiler_params=_params("parallel"),
        name="rope_tables",
    )(positions.reshape(1, n), inv_freq.reshape(half, 1))


def _norm_rope_map(y, gain, cos, sin):
    half = ROPE_DIM // 2
    inv = lax.rsqrt(jnp.mean(y * y, axis=0, keepdims=True) + EPS)
    n = y * inv * gain
    n1 = n[0:half]
    n2 = n[half:ROPE_DIM]
    return jnp.concatenate([n1 * cos - n2 * sin, n2 * cos + n1 * sin, n[ROPE_DIM:]], axis=0)


def _qkv_kernel(x_ref, g_ref, wt_ref, qg_ref, kg_ref, cos_ref, sin_ref, qt_ref, k_ref, vt_ref):
    h = _rms(x_ref[...], g_ref[...]).astype(BF16)
    cos = cos_ref[...]
    sin = sin_ref[...]
    qg = qg_ref[...]
    kg = kg_ref[...]

    def project(r0):
        return lax.dot_general(wt_ref[r0:r0 + QKV_ROW_BLOCK, :], h, _NT_DIMS,
                               preferred_element_type=F32)

    def heads(yt):
        for hd in range(QKV_ROW_BLOCK // V_DIM):
            yield hd * V_DIM, [yt[hd * V_DIM + m * QK_DIM:hd * V_DIM + (m + 1) * QK_DIM]
                               for m in range(V_DIM // QK_DIM)]

    def finish(r0, yt):
        if r0 < D_MODEL:
            for off, maps in heads(yt):
                for m, y in enumerate(maps):
                    rows = slice(r0 + off + m * QK_DIM, r0 + off + (m + 1) * QK_DIM)
                    qt_ref[rows, :] = _norm_rope_map(y, qg, cos, sin).astype(BF16)
        elif r0 < 2 * D_MODEL:
            for off, maps in heads(yt):
                kt = jnp.concatenate([_norm_rope_map(y, kg, cos, sin) for y in maps], axis=0)
                c0 = r0 - D_MODEL + off
                k_ref[:, c0:c0 + V_DIM] = kt.T.astype(BF16)
        else:
            vt_ref[r0 - 2 * D_MODEL:r0 - 2 * D_MODEL + QKV_ROW_BLOCK, :] = yt.astype(BF16)

    starts = list(range(0, 3 * D_MODEL, QKV_ROW_BLOCK))
    nxt = project(starts[0])
    for i, r0 in enumerate(starts):
        cur = nxt
        if i + 1 < len(starts):
            nxt = project(starts[i + 1])
        finish(r0, cur)


def _qkv_proj(x, g, w_t, layer, q_gain, k_gain, cos, sin):
    n = x.shape[0]
    tm = TM_PROJ
    half = ROPE_DIM // 2
    tok = lambda i: (i, 0)
    tok_t = lambda i: (0, i)
    const = lambda i: (0, 0)
    feat_major = jax.ShapeDtypeStruct((D_MODEL, n), BF16)
    return pl.pallas_call(
        _qkv_kernel,
        grid=(n // tm,),
        in_specs=[pl.BlockSpec((tm, D_MODEL), tok),
                  pl.BlockSpec((1, D_MODEL), const),
                  _resident((None, 3 * D_MODEL, D_MODEL), lambda i: (layer, 0, 0)),
                  pl.BlockSpec((QK_DIM, 1), const),
                  pl.BlockSpec((QK_DIM, 1), const),
                  pl.BlockSpec((half, tm), tok_t),
                  pl.BlockSpec((half, tm), tok_t)],
        out_specs=[pl.BlockSpec((D_MODEL, tm), tok_t),
                   pl.BlockSpec((tm, D_MODEL), tok),
                   pl.BlockSpec((D_MODEL, tm), tok_t)],
        out_shape=[feat_major, jax.ShapeDtypeStruct((n, D_MODEL), BF16), feat_major],
        compiler_params=_params("parallel"),
        name="qkv_proj",
    )(x, g, w_t, q_gain, k_gain, cos, sin)


def _attn_kernel(lam_ref, qt_ref, k_ref, vt_ref, sg_ref, o_ref, *, lambda_init, tq, subtract_max):
    lv = lam_ref[...]
    lam = (jnp.exp(jnp.sum(lv[0:1] * lv[1:2], axis=-1, keepdims=True))
           - jnp.exp(jnp.sum(lv[2:3] * lv[3:4], axis=-1, keepdims=True)) + lambda_init)
    sg = sg_ref[...]

    def probabilities(hd, t):
        head = slice(hd * V_DIM, (hd + 1) * V_DIM)
        qt = qt_ref[head, t * tq:(t + 1) * tq]
        zero = jnp.zeros((QK_DIM, tq), BF16)
        q_both = jnp.concatenate([jnp.concatenate([qt[0:QK_DIM], zero], axis=1),
                                  jnp.concatenate([zero, qt[QK_DIM:]], axis=1)], axis=0)
        st = jnp.dot(k_ref[:, head], q_both, preferred_element_type=F32)
        if subtract_max:
            st = st - jnp.max(st, axis=0, keepdims=True)
        p = jnp.exp2(st)
        return p.astype(BF16), jnp.sum(p, axis=0, keepdims=True)

    def finish(hd, t, p, l):
        head = slice(hd * V_DIM, (hd + 1) * V_DIM)
        acc = jnp.dot(vt_ref[head, :], p, preferred_element_type=F32)
        c1 = 1.0 / l[:, 0:tq]
        c2 = lam / l[:, tq:2 * tq]
        ot = acc[:, 0:tq] * c1 - acc[:, tq:2 * tq] * c2
        ms = jnp.mean(ot * ot, axis=0, keepdims=True)
        ot = ot * lax.rsqrt(ms + EPS) * sg * (1.0 - lambda_init)
        o_ref[t * tq:(t + 1) * tq, head] = ot.T.astype(BF16)

    work = [(hd, t) for hd in range(qt_ref.shape[0] // V_DIM) for t in range(qt_ref.shape[1] // tq)]
    nxt = probabilities(*work[0])
    for i, item in enumerate(work):
        cur = nxt
        if i + 1 < len(work):
            nxt = probabilities(*work[i + 1])
        finish(*item, *cur)


def _score_bound(q_gain, k_gain):
    return QK_DIM * jnp.max(jnp.abs(q_gain)) * jnp.max(jnp.abs(k_gain))


def _attention(lam_vecs, qt, k, vt, sub_gain, lambda_init, batch, seq, subtract_max):
    tq = TQ_ATTN
    assert seq % tq == 0 and N_HEADS % ATTN_HEADS_PER_STEP == 0
    width = V_DIM * ATTN_HEADS_PER_STEP
    const = lambda b, h: (0, 0)
    return pl.pallas_call(
        functools.partial(_attn_kernel, lambda_init=lambda_init, tq=tq, subtract_max=subtract_max),
        grid=(batch, N_HEADS // ATTN_HEADS_PER_STEP),
        in_specs=[pl.BlockSpec((4, QK_DIM), const),
                  pl.BlockSpec((width, seq), lambda b, h: (h, b)),
                  pl.BlockSpec((seq, width), lambda b, h: (b, h)),
                  pl.BlockSpec((width, seq), lambda b, h: (h, b)),
                  pl.BlockSpec((V_DIM, 1), const)],
        out_specs=pl.BlockSpec((seq, width), lambda b, h: (b, h)),
        out_shape=jax.ShapeDtypeStruct((batch * seq, D_MODEL), BF16),
        compiler_params=_params("parallel", "parallel"),
        name="diff_attention" if subtract_max else "diff_attention_bounded",
    )(lam_vecs, qt, k, vt, sub_gain)


def _pooled_column_groups(rows, above, below, g, w_ref, sc, t0, seq):
    x = rows()
    n_blk = x.shape[0]
    h = _rms(x, g)
    halo = []
    for neighbour in (above, below):
        nb_rows, in_sequence = neighbour()
        nb_h = _rms(nb_rows, g)
        halo.append(nb_h if in_sequence is True else jnp.where(in_sequence, nb_h, 0.0))
    hh = jnp.concatenate([halo[0], h, halo[1]], axis=0)
    n_rows = n_blk + 2 * POOL_HALO
    t = t0 + lax.broadcasted_iota(jnp.int32, (n_blk, 1), 0)
    for gi, win in enumerate(POOL_WINDOWS):
        r = win // 2
        cols = slice(gi * POOL_GROUP, (gi + 1) * POOL_GROUP)
        run = hh[:, cols]
        span = 1
        while span < win:
            run = run + pltpu.roll(run, span, 0)
            span *= 2
        if r > 1:
            run = pltpu.roll(run, n_rows - (r - 1), 0)
        tot = run[POOL_HALO:POOL_HALO + n_blk, :]
        cnt = (jnp.minimum(t + r, seq) - jnp.maximum(t - r, 0)).astype(F32)
        diff = (tot / cnt - h[:, cols]).astype(BF16)
        y = lax.dot_general(diff, w_ref[gi], _NN_DIMS, preferred_element_type=F32)
        yield x[:, cols] + y * sc[:, cols]


def _mixer_mlp_kernel(*refs, mixer, seq):
    rb = MLP_ROW_BLOCK
    if mixer == "attn":
        x_ref, o_ref, wo_ref, g_ref, w1_ref, w2_ref, out_ref = refs
        tm = x_ref.shape[0]

        def mixed_columns(r0):
            if r0 >= tm:
                return
            rows = slice(r0, r0 + rb)
            yield x_ref[rows, :] + lax.dot_general(o_ref[rows, :], wo_ref[...], _NN_DIMS,
                                                   preferred_element_type=F32)

        first = list(mixed_columns(0))
    else:
        (xp_ref, x_ref, xn_ref, xnext_ref, gmix_ref, wpool_ref, sc_ref, g_ref, w1_ref, w2_ref,
         out_ref, carry_ref) = refs
        tm = x_ref.shape[0]
        gmix = gmix_ref[...]
        sc = sc_ref[...]
        tiles_per_seq = seq // tm
        step = pl.program_id(0)
        j = step % tiles_per_seq
        j_next = (step + 1) % tiles_per_seq

        def mixed_columns(r0):
            if r0 == tm:
                return _pooled_column_groups(
                    lambda: xnext_ref[0:rb, :],
                    lambda: (x_ref[tm - POOL_HALO:tm, :], j_next > 0),
                    lambda: (xnext_ref[rb:rb + POOL_HALO, :], True),
                    gmix, wpool_ref, sc, j_next * tm, seq)
            above = ((lambda: (xp_ref[...], j > 0)) if r0 == 0
                     else (lambda: (x_ref[r0 - POOL_HALO:r0, :], True)))
            below = ((lambda: (xn_ref[...], j < tiles_per_seq - 1)) if r0 + rb == tm
                     else (lambda: (x_ref[r0 + rb:r0 + rb + POOL_HALO, :], True)))
            return _pooled_column_groups(lambda: x_ref[r0:r0 + rb, :], above, below,
                                         gmix, wpool_ref, sc, j * tm + r0, seq)

        @pl.when(step == 0)
        def _():
            carry_ref[...] = jnp.concatenate(list(mixed_columns(0)), axis=1)

        first = [carry_ref[...]]
    g = g_ref[...]
    cur = first
    for r0 in range(0, tm, rb):
        x = jnp.concatenate(cur, axis=1) if len(cur) > 1 else cur[0]
        upcoming = mixed_columns(r0 + rb)
        cur = []
        h = _rms(x, g).astype(BF16)
        acc = x
        for c in range(D_FF // D_MODEL):
            cols = slice(c * D_MODEL, (c + 1) * D_MODEL)
            hid = lax.dot_general(h, w1_ref[:, cols], _NN_DIMS, preferred_element_type=F32)
            act = jnp.square(jnp.maximum(hid, 0.0)).astype(BF16)
            acc = acc + lax.dot_general(act, w2_ref[cols, :], _NN_DIMS, preferred_element_type=F32)
            cur.extend(piece for piece in [next(upcoming, None)] if piece is not None)
        cur.extend(upcoming)
        out_ref[r0:r0 + rb, :] = acc
    if mixer == "pool":
        carry_ref[...] = jnp.concatenate(cur, axis=1)


def _mixer_mlp(x, g, w1, w2, layer, seq, *, attn_out=None, w_o=None, g_mix=None, w_pool=None,
               pool_scale=None, mixer_layer=None):
    n = x.shape[0]
    tm = TM_PROJ
    tok = lambda i: (i, 0)
    const = lambda i: (0, 0)
    if attn_out is not None:
        mixer = "attn"
        operands = [x, attn_out, w_o]
        in_specs = [pl.BlockSpec((tm, D_MODEL), tok),
                    pl.BlockSpec((tm, D_MODEL), tok),
                    _resident((None, D_MODEL, D_MODEL), lambda i: (mixer_layer, 0, 0))]
    else:
        mixer = "pool"
        per_tile = tm // POOL_HALO
        last_halo_block = n // POOL_HALO - 1
        last_tile = n // tm - 1
        operands = [x, x, x, x, g_mix, w_pool, pool_scale]
        in_specs = [pl.BlockSpec((POOL_HALO, D_MODEL), lambda i: (jnp.maximum(i * per_tile - 1, 0), 0)),
                    pl.BlockSpec((tm, D_MODEL), tok),
                    pl.BlockSpec((POOL_HALO, D_MODEL),
                                 lambda i: (jnp.minimum((i + 1) * per_tile, last_halo_block), 0)),
                    pl.BlockSpec((tm, D_MODEL), lambda i: (jnp.minimum(i + 1, last_tile), 0)),
                    pl.BlockSpec((1, D_MODEL), const),
                    _resident((None, len(POOL_WINDOWS), POOL_GROUP, POOL_GROUP),
                              lambda i: (mixer_layer, 0, 0, 0)),
                    pl.BlockSpec((1, D_MODEL), const)]
    operands += [g, w1, w2]
    in_specs += [pl.BlockSpec((1, D_MODEL), const),
                 _resident((None, D_MODEL, D_FF), lambda i: (layer, 0, 0)),
                 _resident((None, D_FF, D_MODEL), lambda i: (layer, 0, 0))]
    return pl.pallas_call(
        functools.partial(_mixer_mlp_kernel, mixer=mixer, seq=seq),
        grid=(n // tm,),
        in_specs=in_specs,
        out_specs=pl.BlockSpec((tm, D_MODEL), tok),
        out_shape=jax.ShapeDtypeStruct((n, D_MODEL), F32),
        scratch_shapes=[] if mixer == "attn" else [pltpu.VMEM((MLP_ROW_BLOCK, D_MODEL), F32)],
        compiler_params=_params("parallel" if mixer == "attn" else "arbitrary"),
        name="out_proj_mlp" if mixer == "attn" else "pool_mixer_mlp",
    )(*operands)


def kernel(x, positions, norm_mix, norm_mlp, attn_w_qkv, attn_w_o, attn_q_gain, attn_k_gain,
           attn_lam_q1, attn_lam_k1, attn_lam_q2, attn_lam_k2, attn_sub_gain,
           pool_w, pool_scale, mlp_w1, mlp_w2):
    batch, seq, d = x.shape
    assert d == D_MODEL and seq % TM_PROJ == 0
    n = batch * seq
    xf = x.reshape(n, d)

    cos, sin = _rope_tables(positions)
    scale = math.log2(math.e) / math.sqrt(QK_DIM)
    w_qkv_t = jnp.swapaxes(attn_w_qkv, 1, 2).astype(BF16)

    for i in range(DEPTH):
        j = i // N_MIXERS
        g_mix = norm_mix[i].reshape(1, d)
        g_mlp = norm_mlp[i].reshape(1, d)
        if i % N_MIXERS == 0:
            q_gain = (attn_q_gain[j] * scale).reshape(QK_DIM, 1)
            k_gain = attn_k_gain[j].reshape(QK_DIM, 1)
            qt, k, vt = _qkv_proj(xf, g_mix, w_qkv_t, j, q_gain, k_gain, cos, sin)
            lam_vecs = jnp.stack([attn_lam_q1[j], attn_lam_k1[j], attn_lam_q2[j], attn_lam_k2[j]])
            attend = functools.partial(_attention, lambda_init=_lambda_init(i), batch=batch, seq=seq)
            o = lax.cond(_score_bound(q_gain, k_gain) <= FAST_SCORE_BOUND,
                         functools.partial(attend, subtract_max=False),
                         functools.partial(attend, subtract_max=True),
                         lam_vecs, qt, k, vt, attn_sub_gain[j].reshape(V_DIM, 1))
            xf = _mixer_mlp(xf, g_mlp, mlp_w1, mlp_w2, i, seq, attn_out=o, w_o=attn_w_o, mixer_layer=j)
        else:
            xf = _mixer_mlp(xf, g_mlp, mlp_w1, mlp_w2, i, seq, g_mix=g_mix, w_pool=pool_w,
                            pool_scale=pool_scale[j].reshape(1, d), mixer_layer=j)
    return xf.reshape(batch, seq, d)
```

```python
import functools
import math

import jax
import jax.numpy as jnp
from jax import lax
from jax.experimental import pallas as pl
from jax.experimental.pallas import tpu as pltpu

D_MODEL = 1024
DEPTH = 4
N_MIXERS = 2
QK_DIM = 64
V_DIM = 2 * QK_DIM
N_HEADS = D_MODEL // V_DIM
ROPE_DIM = QK_DIM // 4
ROPE_THETA = 500000.0
POOL_WINDOWS = (2, 4, 8, 16)
POOL_GROUP = D_MODEL // len(POOL_WINDOWS)
D_FF = 4 * D_MODEL
EPS = 1e-6

LANES = 128
POOL_HALO = 8
FAST_SCORE_BOUND = 50.0
_NT_DIMS = (((1,), (1,)), ((), ()))
_NN_DIMS = (((1,), (0,)), ((), ()))
VMEM_LIMIT = 56 * 1024 * 1024

TM_PROJ = 512
MLP_ROW_BLOCK = 256
QKV_ROW_BLOCK = 512
TQ_ATTN = 256
ATTN_HEADS_PER_STEP = 2
TN_ROPE = 2048

F32 = jnp.float32
BF16 = jnp.bfloat16


def _lambda_init(layer_idx):
    return 0.8 - 0.6 * math.exp(-0.3 * layer_idx)


def _rms(x, g):
    ms = jnp.mean(x * x, axis=-1, keepdims=True)
    return x * lax.rsqrt(ms + EPS) * g


def _params(*sem):
    return pltpu.CompilerParams(dimension_semantics=sem, vmem_limit_bytes=VMEM_LIMIT)


def _resident(block_shape, index_map):
    return pl.BlockSpec(block_shape, index_map, pipeline_mode=pl.Buffered(1))


def _rope_kernel(pos_ref, invf_ref, cos_ref, sin_ref):
    ang = pos_ref[...].astype(F32) * invf_ref[...]
    cos_ref[...] = jnp.cos(ang)
    sin_ref[...] = jnp.sin(ang)


def _rope_tables(positions):
    n = positions.size
    half = ROPE_DIM // 2
    inv_freq = ROPE_THETA ** (-jnp.arange(0, ROPE_DIM, 2, dtype=F32) / ROPE_DIM)
    out = jax.ShapeDtypeStruct((half, n), F32)
    return pl.pallas_call(
        _rope_kernel,
        grid=(n // TN_ROPE,),
        in_specs=[pl.BlockSpec((1, TN_ROPE), lambda i: (0, i)),
                  pl.BlockSpec((half, 1), lambda i: (0, 0))],
        out_specs=[pl.BlockSpec((half, TN_ROPE), lambda i: (0, i))] * 2,
        out_shape=[out, out],
        compiler_params=_params("parallel"),
        name="rope_tables",
    )(positions.reshape(1, n), inv_freq.reshape(half, 1))


def _norm_rope_map(y, gain, cos, sin):
    half = ROPE_DIM // 2
    inv = lax.rsqrt(jnp.mean(y * y, axis=0, keepdims=True) + EPS)
    n = y * inv * gain
    n1 = n[0:half]
    n2 = n[half:ROPE_DIM]
    return jnp.concatenate([n1 * cos - n2 * sin, n2 * cos + n1 * sin, n[ROPE_DIM:]], axis=0)


def _qkv_kernel(x_ref, g_ref, wt_ref, qg_ref, kg_ref, cos_ref, sin_ref, qt_ref, k_ref, vt_ref):
    h = _rms(x_ref[...], g_ref[...]).astype(BF16)
    cos = cos_ref[...]
    sin = sin_ref[...]
    qg = qg_ref[...]
    kg = kg_ref[...]

    def project(r0):
        return lax.dot_general(wt_ref[r0:r0 + QKV_ROW_BLOCK, :], h, _NT_DIMS,
                               preferred_element_type=F32)

    def heads(yt):
        for hd in range(QKV_ROW_BLOCK // V_DIM):
            yield hd * V_DIM, [yt[hd * V_DIM + m * QK_DIM:hd * V_DIM + (m + 1) * QK_DIM]
                               for m in range(V_DIM // QK_DIM)]

    def finish(r0, yt):
        if r0 < D_MODEL:
            for off, maps in heads(yt):
                for m, y in enumerate(maps):
                    rows = slice(r0 + off + m * QK_DIM, r0 + off + (m + 1) * QK_DIM)
                    qt_ref[rows, :] = _norm_rope_map(y, qg, cos, sin).astype(BF16)
        elif r0 < 2 * D_MODEL:
            for off, maps in heads(yt):
                kt = jnp.concatenate([_norm_rope_map(y, kg, cos, sin) for y in maps], axis=0)
                c0 = r0 - D_MODEL + off
                k_ref[:, c0:c0 + V_DIM] = kt.T.astype(BF16)
        else:
            vt_ref[r0 - 2 * D_MODEL:r0 - 2 * D_MODEL + QKV_ROW_BLOCK, :] = yt.astype(BF16)

    starts = list(range(0, 3 * D_MODEL, QKV_ROW_BLOCK))
    nxt = project(starts[0])
    for i, r0 in enumerate(starts):
        cur = nxt
        if i + 1 < len(starts):
            nxt = project(starts[i + 1])
        finish(r0, cur)


def _qkv_proj(x, g, w_t, layer, q_gain, k_gain, cos, sin):
    n = x.shape[0]
    tm = TM_PROJ
    half = ROPE_DIM // 2
    tok = lambda i: (i, 0)
    tok_t = lambda i: (0, i)
    const = lambda i: (0, 0)
    feat_major = jax.ShapeDtypeStruct((D_MODEL, n), BF16)
    return pl.pallas_call(
        _qkv_kernel,
        grid=(n // tm,),
        in_specs=[pl.BlockSpec((tm, D_MODEL), tok),
                  pl.BlockSpec((1, D_MODEL), const),
                  _resident((None, 3 * D_MODEL, D_MODEL), lambda i: (layer, 0, 0)),
                  pl.BlockSpec((QK_DIM, 1), const),
                  pl.BlockSpec((QK_DIM, 1), const),
                  pl.BlockSpec((half, tm), tok_t),
                  pl.BlockSpec((half, tm), tok_t)],
        out_specs=[pl.BlockSpec((D_MODEL, tm), tok_t),
                   pl.BlockSpec((tm, D_MODEL), tok),
                   pl.BlockSpec((D_MODEL, tm), tok_t)],
        out_shape=[feat_major, jax.ShapeDtypeStruct((n, D_MODEL), BF16), feat_major],
        compiler_params=_params("parallel"),
        name="qkv_proj",
    )(x, g, w_t, q_gain, k_gain, cos, sin)


def _attn_kernel(lam_ref, qt_ref, k_ref, vt_ref, sg_ref, o_ref, *, lambda_init, tq, subtract_max):
    lv = lam_ref[...]
    lam = (jnp.exp(jnp.sum(lv[0:1] * lv[1:2], axis=-1, keepdims=True))
           - jnp.exp(jnp.sum(lv[2:3] * lv[3:4], axis=-1, keepdims=True)) + lambda_init)
    sg = sg_ref[...]

    def probabilities(hd, t):
        head = slice(hd * V_DIM, (hd + 1) * V_DIM)
        qt = qt_ref[head, t * tq:(t + 1) * tq]
        zero = jnp.zeros((QK_DIM, tq), BF16)
        q_both = jnp.concatenate([jnp.concatenate([qt[0:QK_DIM], zero], axis=1),
                                  jnp.concatenate([zero, qt[QK_DIM:]], axis=1)], axis=0)
        st = jnp.dot(k_ref[:, head], q_both, preferred_element_type=F32)
        if subtract_max:
            st = st - jnp.max(st, axis=0, keepdims=True)
        p = jnp.exp2(st)
        return p, jnp.sum(p, axis=0, keepdims=True)

    def finish(hd, t, p, l):
        head = slice(hd * V_DIM, (hd + 1) * V_DIM)
        l1 = l[:, 0:tq]
        l2 = l[:, tq:2 * tq]
        a = (p[:, 0:tq] - p[:, tq:2 * tq] * (lam * l1 / l2)).astype(BF16)
        ot = jnp.dot(vt_ref[head, :], a, preferred_element_type=F32) * (1.0 / l1)
        ms = jnp.mean(ot * ot, axis=0, keepdims=True)
        ot = ot * lax.rsqrt(ms + EPS) * sg * (1.0 - lambda_init)
        o_ref[t * tq:(t + 1) * tq, head] = ot.T.astype(BF16)

    work = [(hd, t) for hd in range(qt_ref.shape[0] // V_DIM) for t in range(qt_ref.shape[1] // tq)]
    nxt = probabilities(*work[0])
    for i, item in enumerate(work):
        cur = nxt
        if i + 1 < len(work):
            nxt = probabilities(*work[i + 1])
        finish(*item, *cur)


def _score_bound(q_gain, k_gain):
    return QK_DIM * jnp.max(jnp.abs(q_gain)) * jnp.max(jnp.abs(k_gain))


def _attention(lam_vecs, qt, k, vt, sub_gain, lambda_init, batch, seq, subtract_max):
    tq = TQ_ATTN
    assert seq % tq == 0 and N_HEADS % ATTN_HEADS_PER_STEP == 0
    width = V_DIM * ATTN_HEADS_PER_STEP
    const = lambda b, h: (0, 0)
    return pl.pallas_call(
        functools.partial(_attn_kernel, lambda_init=lambda_init, tq=tq, subtract_max=subtract_max),
        grid=(batch, N_HEADS // ATTN_HEADS_PER_STEP),
        in_specs=[pl.BlockSpec((4, QK_DIM), const),
                  pl.BlockSpec((width, seq), lambda b, h: (h, b)),
                  pl.BlockSpec((seq, width), lambda b, h: (b, h)),
                  pl.BlockSpec((width, seq), lambda b, h: (h, b)),
                  pl.BlockSpec((V_DIM, 1), const)],
        out_specs=pl.BlockSpec((seq, width), lambda b, h: (b, h)),
        out_shape=jax.ShapeDtypeStruct((batch * seq, D_MODEL), BF16),
        compiler_params=_params("parallel", "parallel"),
        name="diff_attention" if subtract_max else "diff_attention_bounded",
    )(lam_vecs, qt, k, vt, sub_gain)


def _pooled_column_groups(rows, above, below, g, w_ref, sc, t0, seq):
    x = rows()
    n_blk = x.shape[0]
    h = _rms(x, g)
    halo = []
    for neighbour in (above, below):
        nb_rows, in_sequence = neighbour()
        nb_h = _rms(nb_rows, g)
        halo.append(nb_h if in_sequence is True else jnp.where(in_sequence, nb_h, 0.0))
    hh = jnp.concatenate([halo[0], h, halo[1]], axis=0)
    n_rows = n_blk + 2 * POOL_HALO
    t = t0 + lax.broadcasted_iota(jnp.int32, (n_blk, 1), 0)
    for gi, win in enumerate(POOL_WINDOWS):
        r = win // 2
        cols = slice(gi * POOL_GROUP, (gi + 1) * POOL_GROUP)
        run = hh[:, cols]
        span = 1
        while span < win:
            run = run + pltpu.roll(run, span, 0)
            span *= 2
        if r > 1:
            run = pltpu.roll(run, n_rows - (r - 1), 0)
        tot = run[POOL_HALO:POOL_HALO + n_blk, :]
        cnt = (jnp.minimum(t + r, seq) - jnp.maximum(t - r, 0)).astype(F32)
        diff = (tot / cnt - h[:, cols]).astype(BF16)
        y = lax.dot_general(diff, w_ref[gi], _NN_DIMS, preferred_element_type=F32)
        yield x[:, cols] + y * sc[:, cols]


def _mixer_mlp_kernel(*refs, mixer, seq):
    rb = MLP_ROW_BLOCK
    if mixer == "attn":
        x_ref, o_ref, wo_ref, g_ref, w1_ref, w2_ref, out_ref = refs
        tm = x_ref.shape[0]

        def mixed_columns(r0):
            if r0 >= tm:
                return
            rows = slice(r0, r0 + rb)
            yield x_ref[rows, :] + lax.dot_general(o_ref[rows, :], wo_ref[...], _NN_DIMS,
                                                   preferred_element_type=F32)

        first = list(mixed_columns(0))
    else:
        (xp_ref, x_ref, xn_ref, xnext_ref, gmix_ref, wpool_ref, sc_ref, g_ref, w1_ref, w2_ref,
         out_ref, carry_ref) = refs
        tm = x_ref.shape[0]
        gmix = gmix_ref[...]
        sc = sc_ref[...]
        tiles_per_seq = seq // tm
        step = pl.program_id(0)
        j = step % tiles_per_seq
        j_next = (step + 1) % tiles_per_seq

        def mixed_columns(r0):
            if r0 == tm:
                return _pooled_column_groups(
                    lambda: xnext_ref[0:rb, :],
                    lambda: (x_ref[tm - POOL_HALO:tm, :], j_next > 0),
                    lambda: (xnext_ref[rb:rb + POOL_HALO, :], True),
                    gmix, wpool_ref, sc, j_next * tm, seq)
            above = ((lambda: (xp_ref[...], j > 0)) if r0 == 0
                     else (lambda: (x_ref[r0 - POOL_HALO:r0, :], True)))
            below = ((lambda: (xn_ref[...], j < tiles_per_seq - 1)) if r0 + rb == tm
                     else (lambda: (x_ref[r0 + rb:r0 + rb + POOL_HALO, :], True)))
            return _pooled_column_groups(lambda: x_ref[r0:r0 + rb, :], above, below,
                                         gmix, wpool_ref, sc, j * tm + r0, seq)

        @pl.when(step == 0)
        def _():
            carry_ref[...] = jnp.concatenate(list(mixed_columns(0)), axis=1)

        first = [carry_ref[...]]
    g = g_ref[...]
    cur = first
    for r0 in range(0, tm, rb):
        x = jnp.concatenate(cur, axis=1) if len(cur) > 1 else cur[0]
        upcoming = mixed_columns(r0 + rb)
        cur = []
        h = _rms(x, g).astype(BF16)
        acc = x
        for c in range(D_FF // D_MODEL):
            cols = slice(c * D_MODEL, (c + 1) * D_MODEL)
            hid = lax.dot_general(h, w1_ref[:, cols], _NN_DIMS, preferred_element_type=F32)
            act = jnp.square(jnp.maximum(hid, 0.0)).astype(BF16)
            acc = acc + lax.dot_general(act, w2_ref[cols, :], _NN_DIMS, preferred_element_type=F32)
            cur.extend(piece for piece in [next(upcoming, None)] if piece is not None)
        cur.extend(upcoming)
        out_ref[r0:r0 + rb, :] = acc
    if mixer == "pool":
        carry_ref[...] = jnp.concatenate(cur, axis=1)


def _mixer_mlp(x, g, w1, w2, layer, seq, *, attn_out=None, w_o=None, g_mix=None, w_pool=None,
               pool_scale=None, mixer_layer=None):
    n = x.shape[0]
    tm = TM_PROJ
    tok = lambda i: (i, 0)
    const = lambda i: (0, 0)
    if attn_out is not None:
        mixer = "attn"
        operands = [x, attn_out, w_o]
        in_specs = [pl.BlockSpec((tm, D_MODEL), tok),
                    pl.BlockSpec((tm, D_MODEL), tok),
                    _resident((None, D_MODEL, D_MODEL), lambda i: (mixer_layer, 0, 0))]
    else:
        mixer = "pool"
        per_tile = tm // POOL_HALO
        last_halo_block = n // POOL_HALO - 1
        last_tile = n // tm - 1
        operands = [x, x, x, x, g_mix, w_pool, pool_scale]
        in_specs = [pl.BlockSpec((POOL_HALO, D_MODEL), lambda i: (jnp.maximum(i * per_tile - 1, 0), 0)),
                    pl.BlockSpec((tm, D_MODEL), tok),
                    pl.BlockSpec((POOL_HALO, D_MODEL),
                                 lambda i: (jnp.minimum((i + 1) * per_tile, last_halo_block), 0)),
                    pl.BlockSpec((tm, D_MODEL), lambda i: (jnp.minimum(i + 1, last_tile), 0)),
                    pl.BlockSpec((1, D_MODEL), const),
                    _resident((None, len(POOL_WINDOWS), POOL_GROUP, POOL_GROUP),
                              lambda i: (mixer_layer, 0, 0, 0)),
                    pl.BlockSpec((1, D_MODEL), const)]
    operands += [g, w1, w2]
    in_specs += [pl.BlockSpec((1, D_MODEL), const),
                 _resident((None, D_MODEL, D_FF), lambda i: (layer, 0, 0)),
                 _resident((None, D_FF, D_MODEL), lambda i: (layer, 0, 0))]
    return pl.pallas_call(
        functools.partial(_mixer_mlp_kernel, mixer=mixer, seq=seq),
        grid=(n // tm,),
        in_specs=in_specs,
        out_specs=pl.BlockSpec((tm, D_MODEL), tok),
        out_shape=jax.ShapeDtypeStruct((n, D_MODEL), F32),
        scratch_shapes=[] if mixer == "attn" else [pltpu.VMEM((MLP_ROW_BLOCK, D_MODEL), F32)],
        compiler_params=_params("parallel" if mixer == "attn" else "arbitrary"),
        name="out_proj_mlp" if mixer == "attn" else "pool_mixer_mlp",
    )(*operands)


def kernel(x, positions, norm_mix, norm_mlp, attn_w_qkv, attn_w_o, attn_q_gain, attn_k_gain,
           attn_lam_q1, attn_lam_k1, attn_lam_q2, attn_lam_k2, attn_sub_gain,
           pool_w, pool_scale, mlp_w1, mlp_w2):
    batch, seq, d = x.shape
    assert d == D_MODEL and seq % TM_PROJ == 0
    n = batch * seq
    xf = x.reshape(n, d)

    cos, sin = _rope_tables(positions)
    scale = math.log2(math.e) / math.sqrt(QK_DIM)
    w_qkv_t = jnp.swapaxes(attn_w_qkv, 1, 2).astype(BF16)

    for i in range(DEPTH):
        j = i // N_MIXERS
        g_mix = norm_mix[i].reshape(1, d)
        g_mlp = norm_mlp[i].reshape(1, d)
        if i % N_MIXERS == 0:
            q_gain = (attn_q_gain[j] * scale).reshape(QK_DIM, 1)
            k_gain = attn_k_gain[j].reshape(QK_DIM, 1)
            qt, k, vt = _qkv_proj(xf, g_mix, w_qkv_t, j, q_gain, k_gain, cos, sin)
            lam_vecs = jnp.stack([attn_lam_q1[j], attn_lam_k1[j], attn_lam_q2[j], attn_lam_k2[j]])
            attend = functools.partial(_attention, lambda_init=_lambda_init(i), batch=batch, seq=seq)
            o = lax.cond(_score_bound(q_gain, k_gain) <= FAST_SCORE_BOUND,
                         functools.partial(attend, subtract_max=False),
                         functools.partial(attend, subtract_max=True),
                         lam_vecs, qt, k, vt, attn_sub_gain[j].reshape(V_DIM, 1))
            xf = _mixer_mlp(xf, g_mlp, mlp_w1, mlp_w2, i, seq, attn_out=o, w_o=attn_w_o, mixer_layer=j)
        else:
            xf = _mixer_mlp(xf, g_mlp, mlp_w1, mlp_w2, i, seq, g_mix=g_mix, w_pool=pool_w,
                            pool_scale=pool_scale[j].reshape(1, d), mixer_layer=j)
    return xf.reshape(batch, seq, d)
```

```python
import functools
import math

import jax
import jax.numpy as jnp
from jax import lax
from jax.experimental import pallas as pl
from jax.experimental.pallas import tpu as pltpu

D_MODEL = 1024
DEPTH = 4
N_MIXERS = 2
QK_DIM = 64
V_DIM = 2 * QK_DIM
N_HEADS = D_MODEL // V_DIM
ROPE_DIM = QK_DIM // 4
ROPE_THETA = 500000.0
POOL_WINDOWS = (2, 4, 8, 16)
POOL_GROUP = D_MODEL // len(POOL_WINDOWS)
D_FF = 4 * D_MODEL
EPS = 1e-6

POOL_HALO = 8
FAST_SCORE_BOUND = 50.0
_NT_DIMS = (((1,), (1,)), ((), ()))
_NN_DIMS = (((1,), (0,)), ((), ()))
VMEM_LIMIT = 56 * 1024 * 1024

TM_PROJ = 512
MLP_ROW_BLOCK = 256
QKV_ROW_BLOCK = 512
TQ_ATTN = 256
ATTN_HEADS_PER_STEP = 2

F32 = jnp.float32
BF16 = jnp.bfloat16


def _lambda_init(layer_idx):
    return 0.8 - 0.6 * math.exp(-0.3 * layer_idx)


def _rms(x, g):
    ms = jnp.mean(x * x, axis=-1, keepdims=True)
    return x * lax.rsqrt(ms + EPS) * g


def _params(*sem):
    return pltpu.CompilerParams(dimension_semantics=sem, vmem_limit_bytes=VMEM_LIMIT)


def _resident(block_shape, index_map):
    return pl.BlockSpec(block_shape, index_map, pipeline_mode=pl.Buffered(1))


def _norm_rope_map(y, gain, cos, sin):
    half = ROPE_DIM // 2
    inv = lax.rsqrt(jnp.mean(y * y, axis=0, keepdims=True) + EPS)
    n = y * inv * gain
    n1 = n[0:half]
    n2 = n[half:ROPE_DIM]
    return jnp.concatenate([n1 * cos - n2 * sin, n2 * cos + n1 * sin, n[ROPE_DIM:]], axis=0)


def _qkv_kernel(x_ref, g_ref, wt_ref, qg_ref, kg_ref, pos_ref, invf_ref, qt_ref, k_ref, vt_ref):
    h = _rms(x_ref[...], g_ref[...]).astype(BF16)
    ang = pos_ref[...].astype(F32) * invf_ref[...]
    cos = jnp.cos(ang)
    sin = jnp.sin(ang)
    qg = qg_ref[...]
    kg = kg_ref[...]

    def project(r0):
        return lax.dot_general(wt_ref[r0:r0 + QKV_ROW_BLOCK, :], h, _NT_DIMS,
                               preferred_element_type=F32)

    def heads(yt):
        for hd in range(QKV_ROW_BLOCK // V_DIM):
            yield hd * V_DIM, [yt[hd * V_DIM + m * QK_DIM:hd * V_DIM + (m + 1) * QK_DIM]
                               for m in range(V_DIM // QK_DIM)]

    def finish(r0, yt):
        if r0 < D_MODEL:
            for off, maps in heads(yt):
                for m, y in enumerate(maps):
                    rows = slice(r0 + off + m * QK_DIM, r0 + off + (m + 1) * QK_DIM)
                    qt_ref[rows, :] = _norm_rope_map(y, qg, cos, sin).astype(BF16)
        elif r0 < 2 * D_MODEL:
            for off, maps in heads(yt):
                kt = jnp.concatenate([_norm_rope_map(y, kg, cos, sin) for y in maps], axis=0)
                c0 = r0 - D_MODEL + off
                k_ref[:, c0:c0 + V_DIM] = kt.T.astype(BF16)
        else:
            vt_ref[r0 - 2 * D_MODEL:r0 - 2 * D_MODEL + QKV_ROW_BLOCK, :] = yt.astype(BF16)

    starts = list(range(0, 3 * D_MODEL, QKV_ROW_BLOCK))
    nxt = project(starts[0])
    for i, r0 in enumerate(starts):
        cur = nxt
        if i + 1 < len(starts):
            nxt = project(starts[i + 1])
        finish(r0, cur)


def _qkv_proj(x, g, w_t, layer, q_gain, k_gain, positions, inv_freq):
    n = x.shape[0]
    tm = TM_PROJ
    half = ROPE_DIM // 2
    tok = lambda i: (i, 0)
    tok_t = lambda i: (0, i)
    const = lambda i: (0, 0)
    feat_major = jax.ShapeDtypeStruct((D_MODEL, n), BF16)
    return pl.pallas_call(
        _qkv_kernel,
        grid=(n // tm,),
        in_specs=[pl.BlockSpec((tm, D_MODEL), tok),
                  pl.BlockSpec((1, D_MODEL), const),
                  _resident((None, 3 * D_MODEL, D_MODEL), lambda i: (layer, 0, 0)),
                  pl.BlockSpec((QK_DIM, 1), const),
                  pl.BlockSpec((QK_DIM, 1), const),
                  pl.BlockSpec((1, tm), tok_t),
                  pl.BlockSpec((half, 1), const)],
        out_specs=[pl.BlockSpec((D_MODEL, tm), tok_t),
                   pl.BlockSpec((tm, D_MODEL), tok),
                   pl.BlockSpec((D_MODEL, tm), tok_t)],
        out_shape=[feat_major, jax.ShapeDtypeStruct((n, D_MODEL), BF16), feat_major],
        compiler_params=_params("parallel"),
        name="qkv_proj",
    )(x, g, w_t, q_gain, k_gain, positions, inv_freq)


def _attn_body(lam_ref, qt_ref, k_ref, vt_ref, sg_ref, o_ref, *, lambda_init, tq, subtract_max):
    lv = lam_ref[...]
    lam = (jnp.exp(jnp.sum(lv[0:1] * lv[1:2], axis=-1, keepdims=True))
           - jnp.exp(jnp.sum(lv[2:3] * lv[3:4], axis=-1, keepdims=True)) + lambda_init)
    sg = sg_ref[...]

    def probabilities(hd, t):
        head = slice(hd * V_DIM, (hd + 1) * V_DIM)
        qt = qt_ref[head, t * tq:(t + 1) * tq]
        zero = jnp.zeros((QK_DIM, tq), BF16)
        q_both = jnp.concatenate([jnp.concatenate([qt[0:QK_DIM], zero], axis=1),
                                  jnp.concatenate([zero, qt[QK_DIM:]], axis=1)], axis=0)
        st = jnp.dot(k_ref[:, head], q_both, preferred_element_type=F32)
        if subtract_max:
            st = st - jnp.max(st, axis=0, keepdims=True)
        p = jnp.exp2(st)
        return p, jnp.sum(p, axis=0, keepdims=True)

    def finish(hd, t, p, l):
        head = slice(hd * V_DIM, (hd + 1) * V_DIM)
        l1 = l[:, 0:tq]
        l2 = l[:, tq:2 * tq]
        a = (p[:, 0:tq] - p[:, tq:2 * tq] * (lam * l1 / l2)).astype(BF16)
        ot = jnp.dot(vt_ref[head, :], a, preferred_element_type=F32) * (1.0 / l1)
        ms = jnp.mean(ot * ot, axis=0, keepdims=True)
        ot = ot * lax.rsqrt(ms + EPS) * sg * (1.0 - lambda_init)
        o_ref[t * tq:(t + 1) * tq, head] = ot.T.astype(BF16)

    work = [(hd, t) for hd in range(qt_ref.shape[0] // V_DIM) for t in range(qt_ref.shape[1] // tq)]
    nxt = probabilities(*work[0])
    for i, item in enumerate(work):
        cur = nxt
        if i + 1 < len(work):
            nxt = probabilities(*work[i + 1])
        finish(*item, *cur)


def _attn_kernel(bounded_ref, *refs, lambda_init, tq):
    bounded = bounded_ref[0]

    @pl.when(bounded != 0)
    def _():
        _attn_body(*refs, lambda_init=lambda_init, tq=tq, subtract_max=False)

    @pl.when(bounded == 0)
    def _():
        _attn_body(*refs, lambda_init=lambda_init, tq=tq, subtract_max=True)


def _score_bound(q_gain, k_gain):
    return QK_DIM * jnp.max(jnp.abs(q_gain)) * jnp.max(jnp.abs(k_gain))


def _attention(bounded, lam_vecs, qt, k, vt, sub_gain, lambda_init, batch, seq):
    tq = TQ_ATTN
    assert seq % tq == 0 and N_HEADS % ATTN_HEADS_PER_STEP == 0
    width = V_DIM * ATTN_HEADS_PER_STEP
    const = lambda b, h: (0, 0)
    return pl.pallas_call(
        functools.partial(_attn_kernel, lambda_init=lambda_init, tq=tq),
        grid=(batch, N_HEADS // ATTN_HEADS_PER_STEP),
        in_specs=[pl.BlockSpec(memory_space=pltpu.SMEM),
                  pl.BlockSpec((4, QK_DIM), const),
                  pl.BlockSpec((width, seq), lambda b, h: (h, b)),
                  pl.BlockSpec((seq, width), lambda b, h: (b, h)),
                  pl.BlockSpec((width, seq), lambda b, h: (h, b)),
                  pl.BlockSpec((V_DIM, 1), const)],
        out_specs=pl.BlockSpec((seq, width), lambda b, h: (b, h)),
        out_shape=jax.ShapeDtypeStruct((batch * seq, D_MODEL), BF16),
        compiler_params=_params("parallel", "parallel"),
        name="diff_attention",
    )(bounded, lam_vecs, qt, k, vt, sub_gain)


def _pooled_column_groups(rows, above, below, g, w_ref, sc, t0, seq):
    x = rows()
    n_blk = x.shape[0]
    h = _rms(x, g)
    halo = []
    for neighbour in (above, below):
        nb_rows, in_sequence = neighbour()
        nb_h = _rms(nb_rows, g)
        halo.append(nb_h if in_sequence is True else jnp.where(in_sequence, nb_h, 0.0))
    hh = jnp.concatenate([halo[0], h, halo[1]], axis=0)
    n_rows = n_blk + 2 * POOL_HALO
    t = t0 + lax.broadcasted_iota(jnp.int32, (n_blk, 1), 0)
    for gi, win in enumerate(POOL_WINDOWS):
        r = win // 2
        cols = slice(gi * POOL_GROUP, (gi + 1) * POOL_GROUP)
        run = hh[:, cols]
        span = 1
        while span < win:
            run = run + pltpu.roll(run, span, 0)
            span *= 2
        if r > 1:
            run = pltpu.roll(run, n_rows - (r - 1), 0)
        tot = run[POOL_HALO:POOL_HALO + n_blk, :]
        cnt = (jnp.minimum(t + r, seq) - jnp.maximum(t - r, 0)).astype(F32)
        diff = (tot / cnt - h[:, cols]).astype(BF16)
        y = lax.dot_general(diff, w_ref[gi], _NN_DIMS, preferred_element_type=F32)
        yield x[:, cols] + y * sc[:, cols]


def _mixer_mlp_kernel(*refs, mixer, seq):
    rb = MLP_ROW_BLOCK
    if mixer == "attn":
        x_ref, o_ref, wo_ref, g_ref, w1_ref, w2_ref, out_ref = refs
        tm = x_ref.shape[0]

        def mixed_columns(r0):
            if r0 >= tm:
                return
            rows = slice(r0, r0 + rb)
            yield x_ref[rows, :] + lax.dot_general(o_ref[rows, :], wo_ref[...], _NN_DIMS,
                                                   preferred_element_type=F32)

        first = list(mixed_columns(0))
    else:
        (xp_ref, x_ref, xn_ref, xnext_ref, gmix_ref, wpool_ref, sc_ref, g_ref, w1_ref, w2_ref,
         out_ref, carry_ref) = refs
        tm = x_ref.shape[0]
        gmix = gmix_ref[...]
        sc = sc_ref[...]
        tiles_per_seq = seq // tm
        step = pl.program_id(0)
        j = step % tiles_per_seq
        j_next = (step + 1) % tiles_per_seq

        def mixed_columns(r0):
            if r0 == tm:
                return _pooled_column_groups(
                    lambda: xnext_ref[0:rb, :],
                    lambda: (x_ref[tm - POOL_HALO:tm, :], j_next > 0),
                    lambda: (xnext_ref[rb:rb + POOL_HALO, :], True),
                    gmix, wpool_ref, sc, j_next * tm, seq)
            above = ((lambda: (xp_ref[...], j > 0)) if r0 == 0
                     else (lambda: (x_ref[r0 - POOL_HALO:r0, :], True)))
            below = ((lambda: (xn_ref[...], j < tiles_per_seq - 1)) if r0 + rb == tm
                     else (lambda: (x_ref[r0 + rb:r0 + rb + POOL_HALO, :], True)))
            return _pooled_column_groups(lambda: x_ref[r0:r0 + rb, :], above, below,
                                         gmix, wpool_ref, sc, j * tm + r0, seq)

        @pl.when(step == 0)
        def _():
            carry_ref[...] = jnp.concatenate(list(mixed_columns(0)), axis=1)

        first = [carry_ref[...]]
    g = g_ref[...]
    cur = first
    for r0 in range(0, tm, rb):
        x = jnp.concatenate(cur, axis=1) if len(cur) > 1 else cur[0]
        upcoming = mixed_columns(r0 + rb)
        cur = []
        h = _rms(x, g).astype(BF16)
        acc = x
        for c in range(D_FF // D_MODEL):
            cols = slice(c * D_MODEL, (c + 1) * D_MODEL)
            hid = lax.dot_general(h, w1_ref[:, cols], _NN_DIMS, preferred_element_type=F32)
            act = jnp.square(jnp.maximum(hid, 0.0)).astype(BF16)
            acc = acc + lax.dot_general(act, w2_ref[cols, :], _NN_DIMS, preferred_element_type=F32)
            cur.extend(piece for piece in [next(upcoming, None)] if piece is not None)
        cur.extend(upcoming)
        out_ref[r0:r0 + rb, :] = acc
    if mixer == "pool":
        carry_ref[...] = jnp.concatenate(cur, axis=1)


def _mixer_mlp(x, g, w1, w2, layer, seq, *, attn_out=None, w_o=None, g_mix=None, w_pool=None,
               pool_scale=None, mixer_layer=None):
    n = x.shape[0]
    tm = TM_PROJ
    tok = lambda i: (i, 0)
    const = lambda i: (0, 0)
    if attn_out is not None:
        mixer = "attn"
        operands = [x, attn_out, w_o]
        in_specs = [pl.BlockSpec((tm, D_MODEL), tok),
                    pl.BlockSpec((tm, D_MODEL), tok),
                    _resident((None, D_MODEL, D_MODEL), lambda i: (mixer_layer, 0, 0))]
    else:
        mixer = "pool"
        per_tile = tm // POOL_HALO
        last_halo_block = n // POOL_HALO - 1
        last_tile = n // tm - 1
        operands = [x, x, x, x, g_mix, w_pool, pool_scale]
        in_specs = [pl.BlockSpec((POOL_HALO, D_MODEL), lambda i: (jnp.maximum(i * per_tile - 1, 0), 0)),
                    pl.BlockSpec((tm, D_MODEL), tok),
                    pl.BlockSpec((POOL_HALO, D_MODEL),
                                 lambda i: (jnp.minimum((i + 1) * per_tile, last_halo_block), 0)),
                    pl.BlockSpec((tm, D_MODEL), lambda i: (jnp.minimum(i + 1, last_tile), 0)),
                    pl.BlockSpec((1, D_MODEL), const),
                    _resident((None, len(POOL_WINDOWS), POOL_GROUP, POOL_GROUP),
                              lambda i: (mixer_layer, 0, 0, 0)),
                    pl.BlockSpec((1, D_MODEL), const)]
    operands += [g, w1, w2]
    in_specs += [pl.BlockSpec((1, D_MODEL), const),
                 _resident((None, D_MODEL, D_FF), lambda i: (layer, 0, 0)),
                 _resident((None, D_FF, D_MODEL), lambda i: (layer, 0, 0))]
    return pl.pallas_call(
        functools.partial(_mixer_mlp_kernel, mixer=mixer, seq=seq),
        grid=(n // tm,),
        in_specs=in_specs,
        out_specs=pl.BlockSpec((tm, D_MODEL), tok),
        out_shape=jax.ShapeDtypeStruct((n, D_MODEL), F32),
        scratch_shapes=[] if mixer == "attn" else [pltpu.VMEM((MLP_ROW_BLOCK, D_MODEL), F32)],
        compiler_params=_params("parallel" if mixer == "attn" else "arbitrary"),
        name="out_proj_mlp" if mixer == "attn" else "pool_mixer_mlp",
    )(*operands)


def kernel(x, positions, norm_mix, norm_mlp, attn_w_qkv, attn_w_o, attn_q_gain, attn_k_gain,
           attn_lam_q1, attn_lam_k1, attn_lam_q2, attn_lam_k2, attn_sub_gain,
           pool_w, pool_scale, mlp_w1, mlp_w2):
    batch, seq, d = x.shape
    assert d == D_MODEL and seq % TM_PROJ == 0
    n = batch * seq
    xf = x.reshape(n, d)

    inv_freq = (ROPE_THETA ** (-jnp.arange(0, ROPE_DIM, 2, dtype=F32) / ROPE_DIM)).reshape(ROPE_DIM // 2, 1)
    pos_row = positions.reshape(1, n)
    scale = math.log2(math.e) / math.sqrt(QK_DIM)
    w_qkv_t = jnp.swapaxes(attn_w_qkv, 1, 2).astype(BF16)

    for i in range(DEPTH):
        j = i // N_MIXERS
        g_mix = norm_mix[i].reshape(1, d)
        g_mlp = norm_mlp[i].reshape(1, d)
        if i % N_MIXERS == 0:
            q_gain = (attn_q_gain[j] * scale).reshape(QK_DIM, 1)
            k_gain = attn_k_gain[j].reshape(QK_DIM, 1)
            qt, k, vt = _qkv_proj(xf, g_mix, w_qkv_t, j, q_gain, k_gain, pos_row, inv_freq)
            lam_vecs = jnp.stack([attn_lam_q1[j], attn_lam_k1[j], attn_lam_q2[j], attn_lam_k2[j]])
            bounded = (_score_bound(q_gain, k_gain) <= FAST_SCORE_BOUND).astype(jnp.int32).reshape(1)
            o = _attention(bounded, lam_vecs, qt, k, vt, attn_sub_gain[j].reshape(V_DIM, 1),
                           _lambda_init(i), batch, seq)
            xf = _mixer_mlp(xf, g_mlp, mlp_w1, mlp_w2, i, seq, attn_out=o, w_o=attn_w_o, mixer_layer=j)
        else:
            xf = _mixer_mlp(xf, g_mlp, mlp_w1, mlp_w2, i, seq, g_mix=g_mix, w_pool=pool_w,
                            pool_scale=pool_scale[j].reshape(1, d), mixer_layer=j)
    return xf.reshape(batch, seq, d)
```
